```python
import jax, jax.numpy as jnp
from jax import lax
import numpy as np

D_MODEL = 1024
BATCH = 1
SEQ = 16384
DEPTH = 2
DEC_BATCH = 32
DEC_SEQ = 4
PAST_LEN = 16384
PAGE_SIZE = 128

HEAD_DIM = 64
N_DSA_HEADS = 8
N_DSA_KV_HEADS = 2
DSA_GROUP = N_DSA_HEADS // N_DSA_KV_HEADS
N_FOX_HEADS = 8
N_IDX_HEADS = 4
IDX_DIM = 64
ROPE_THETA = 500000.0
MAX_TOPK = 256
D_FF = 4 * D_MODEL
Q_BLOCK = 128
EPS = 1e-6
FORGET_BIAS_INIT = 2.0

DSA_Q = N_DSA_HEADS * HEAD_DIM
DSA_KV = N_DSA_KV_HEADS * HEAD_DIM
IDX_Q = N_IDX_HEADS * IDX_DIM
FOX_W = N_FOX_HEADS * HEAD_DIM
MIX_WIDTH = DSA_Q + FOX_W
SPLITS = (DSA_Q, DSA_KV, DSA_KV, IDX_Q, IDX_DIM, N_IDX_HEADS, FOX_W, FOX_W, FOX_W, N_FOX_HEADS)
D_IN = DSA_Q + 2 * DSA_KV + IDX_Q + IDX_DIM + N_IDX_HEADS + 3 * FOX_W + N_FOX_HEADS

kernel_name = 'hymba_dsa_fox_decoder_step'


def rmsnorm(x, g):
    xf = x.astype(jnp.float32)
    y = xf * lax.rsqrt(jnp.mean(xf * xf, axis=-1, keepdims=True) + EPS)
    return (y * g.astype(jnp.float32)).astype(x.dtype)


def rope_partial(x, pos):
    d = x.shape[-1]
    r = d // 4
    half = r // 2
    inv_freq = jnp.power(ROPE_THETA, -jnp.arange(half, dtype=jnp.float32) * 2.0 / r)
    ang = pos.astype(jnp.float32)[:, None] * inv_freq[None, :]
    cos = jnp.cos(ang)[:, None, :]
    sin = jnp.sin(ang)[:, None, :]
    xf = x.astype(jnp.float32)
    x1 = xf[..., :half]
    x2 = xf[..., half:r]
    out = jnp.concatenate([x1 * cos - x2 * sin, x2 * cos + x1 * sin, xf[..., r:]], axis=-1)
    return out.astype(x.dtype)


def project(hn, w_in, b_f, pos):
    b, s, _ = hn.shape
    cuts = np.cumsum(SPLITS)[:-1].tolist()
    q, k, v, iq, ik, iw, fq, fk, fv, fg = jnp.split(jnp.einsum('bsd,de->bse', hn, w_in), cuts, axis=-1)
    q = rope_partial(q.reshape(b, s, N_DSA_HEADS, HEAD_DIM), pos)
    k = rope_partial(k.reshape(b, s, N_DSA_KV_HEADS, HEAD_DIM), pos)
    v = v.reshape(b, s, N_DSA_KV_HEADS, HEAD_DIM)
    iq = rope_partial(iq.reshape(b, s, N_IDX_HEADS, IDX_DIM), pos)
    ik = rope_partial(ik.reshape(b, s, 1, IDX_DIM), pos)[:, :, 0]
    iw = iw * (N_IDX_HEADS ** -0.5)
    fq = fq.reshape(b, s, N_FOX_HEADS, HEAD_DIM)
    fk = fk.reshape(b, s, N_FOX_HEADS, HEAD_DIM)
    fv = fv.reshape(b, s, N_FOX_HEADS, HEAD_DIM)
    logf = jax.nn.log_sigmoid(fg.astype(jnp.float32) + b_f.astype(jnp.float32))
    return q, k, v, iq, ik, iw, fq, fk, fv, logf


def dsa_attend(q, iq, iw, qpos, k_all, v_all, ik_all, kpos, topk):
    rel = jax.nn.relu(jnp.einsum('bqhc,blc->bqhl', iq, ik_all).astype(jnp.float32))
    score = jnp.einsum('bqh,bqhl->bql', iw.astype(jnp.float32), rel)
    causal = kpos[None, :] <= qpos[:, None]
    score = jnp.where(causal[None], score, -jnp.inf)
    _, idx = lax.top_k(score, topk)
    valid = jnp.take(kpos, idx) <= qpos[None, :, None]
    kg = jax.vmap(lambda a, i: a[i])(k_all, idx)
    vg = jax.vmap(lambda a, i: a[i])(v_all, idx)
    b, nq = q.shape[0], q.shape[1]
    qg = q.reshape(b, nq, N_DSA_KV_HEADS, DSA_GROUP, HEAD_DIM)
    logits = jnp.einsum('bqngd,bqknd->bngqk', qg, kg).astype(jnp.float32) * (HEAD_DIM ** -0.5)
    logits = jnp.where(valid[:, None, None], logits, -jnp.inf)
    p = jax.nn.softmax(logits, axis=-1)
    o = jnp.einsum('bngqk,bqknd->bqngd', p.astype(vg.dtype), vg)
    return o.reshape(b, nq, DSA_Q)


def fox_attend(q, k, v, c_q, c_k, qpos, kpos):
    logits = jnp.einsum('bqhd,blhd->bhql', q, k).astype(jnp.float32) * (HEAD_DIM ** -0.5)
    bias = jnp.swapaxes(c_q, 1, 2)[:, :, :, None] - jnp.swapaxes(c_k, 1, 2)[:, :, None, :]
    causal = kpos[None, :] <= qpos[:, None]
    logits = jnp.where(causal[None, None], logits + bias, -jnp.inf)
    p = jax.nn.softmax(logits, axis=-1)
    o = jnp.einsum('bhql,blhd->bqhd', p.astype(v.dtype), v)
    return o.reshape(o.shape[0], o.shape[1], FOX_W)


def squared_relu_mlp(h, w_up, w_down):
    a = jax.nn.relu(jnp.einsum('bsd,df->bsf', h, w_up))
    return jnp.einsum('bsf,fd->bsd', a * a, w_down)


def gather_pages(cache_l, page_table):
    g = cache_l[page_table]
    return g.reshape((g.shape[0], g.shape[1] * g.shape[2]) + g.shape[3:])


def setup_inputs(seed: int = 0) -> dict:
    key = jax.random.key(seed)
    ks = jax.random.split(key, 20)
    f32 = jnp.float32
    n_pages = PAST_LEN // PAGE_SIZE
    used = DEC_BATCH * n_pages
    n_pool = used + (used + 3) // 4

    def nrm(k, shape, scale=1.0):
        return jax.random.normal(k, shape, f32) * scale

    x_prompt = nrm(ks[0], (BATCH, SEQ, D_MODEL))
    x_sample = nrm(ks[1], (DEC_BATCH, DEC_SEQ, D_MODEL))
    cache_dsa_k = nrm(ks[2], (DEPTH, n_pool, PAGE_SIZE, N_DSA_KV_HEADS, HEAD_DIM))
    cache_dsa_v = nrm(ks[3], (DEPTH, n_pool, PAGE_SIZE, N_DSA_KV_HEADS, HEAD_DIM))
    cache_idx_k = nrm(ks[4], (DEPTH, n_pool, PAGE_SIZE, IDX_DIM))
    cache_fox_k = nrm(ks[5], (DEPTH, n_pool, PAGE_SIZE, N_FOX_HEADS, HEAD_DIM))
    cache_fox_v = nrm(ks[6], (DEPTH, n_pool, PAGE_SIZE, N_FOX_HEADS, HEAD_DIM))
    cache_fox_logf = jax.nn.log_sigmoid(nrm(ks[7], (DEPTH, n_pool, PAGE_SIZE, N_FOX_HEADS)) + FORGET_BIAS_INIT)
    page_table = jax.random.permutation(ks[8], n_pool)[:used].reshape(DEC_BATCH, n_pages).astype(jnp.int32)
    attn_norm = 1.0 + 0.05 * nrm(ks[9], (DEPTH, D_MODEL))
    w_in = nrm(ks[10], (DEPTH, D_MODEL, D_IN), D_MODEL ** -0.5)
    b_f = FORGET_BIAS_INIT + 0.1 * nrm(ks[11], (DEPTH, N_FOX_HEADS))
    w_o = nrm(ks[12], (DEPTH, MIX_WIDTH, D_MODEL), MIX_WIDTH ** -0.5)
    mlp_norm = 1.0 + 0.05 * nrm(ks[13], (DEPTH, D_MODEL))
    w_up = nrm(ks[14], (DEPTH, D_MODEL, D_FF), D_MODEL ** -0.5)
    w_down = nrm(ks[15], (DEPTH, D_FF, D_MODEL), D_FF ** -0.5)
    final_norm = 1.0 + 0.05 * nrm(ks[16], (D_MODEL,))
    return {'x_prompt': x_prompt, 'x_sample': x_sample,
            'cache_dsa_k': cache_dsa_k, 'cache_dsa_v': cache_dsa_v, 'cache_idx_k': cache_idx_k,
            'cache_fox_k': cache_fox_k, 'cache_fox_v': cache_fox_v, 'cache_fox_logf': cache_fox_logf,
            'page_table': page_table,
            'attn_norm': attn_norm, 'w_in': w_in, 'b_f': b_f, 'w_o': w_o,
            'mlp_norm': mlp_norm, 'w_up': w_up, 'w_down': w_down, 'final_norm': final_norm}


def reference(x_prompt, x_sample, cache_dsa_k, cache_dsa_v, cache_idx_k, cache_fox_k, cache_fox_v,
              cache_fox_logf, page_table, attn_norm, w_in, b_f, w_o, mlp_norm, w_up, w_down, final_norm):
    B, S, _ = x_prompt.shape
    T = x_sample.shape[1]
    past_len = page_table.shape[1] * cache_dsa_k.shape[2]
    pos_p = jnp.arange(S, dtype=jnp.int32)
    pos_s = past_len + jnp.arange(T, dtype=jnp.int32)
    pos_all = jnp.arange(past_len + T, dtype=jnp.int32)
    topk_p = min(MAX_TOPK, S // 4)
    topk_s = min(MAX_TOPK, (past_len + T) // 4)
    n_blocks = S // Q_BLOCK

    hp = x_prompt
    hs = x_sample
    p_dsa_k, p_dsa_v, p_idx_k, p_fox_k, p_fox_v, p_fox_logf = [], [], [], [], [], []
    s_dsa_k, s_dsa_v, s_idx_k, s_fox_k, s_fox_v, s_fox_logf = [], [], [], [], [], []
    for l in range(DEPTH):
        q, k, v, iq, ik, iw, fq, fk, fv, logf = project(rmsnorm(hp, attn_norm[l]), w_in[l], b_f[l], pos_p)
        cum = jnp.cumsum(logf, axis=1)

        def prompt_block(i):
            s0 = i * Q_BLOCK
            sl = lambda a: lax.dynamic_slice_in_dim(a, s0, Q_BLOCK, axis=1)
            qpos = s0 + jnp.arange(Q_BLOCK, dtype=jnp.int32)
            o_dsa = dsa_attend(sl(q), sl(iq), sl(iw), qpos, k, v, ik, pos_p, topk_p)
            o_fox = fox_attend(sl(fq), fk, fv, sl(cum), cum, qpos, pos_p)
            return jnp.concatenate([o_dsa, o_fox], axis=-1)

        mix = lax.map(prompt_block, jnp.arange(n_blocks, dtype=jnp.int32))
        mix = jnp.swapaxes(mix, 0, 1).reshape(B, S, MIX_WIDTH)
        hp = hp + jnp.einsum('bsm,md->bsd', mix, w_o[l])
        hp = hp + squared_relu_mlp(rmsnorm(hp, mlp_norm[l]), w_up[l], w_down[l])
        p_dsa_k.append(k); p_dsa_v.append(v); p_idx_k.append(ik)
        p_fox_k.append(fk); p_fox_v.append(fv); p_fox_logf.append(logf)

        q, k, v, iq, ik, iw, fq, fk, fv, logf = project(rmsnorm(hs, attn_norm[l]), w_in[l], b_f[l], pos_s)
        k_all = jnp.concatenate([gather_pages(cache_dsa_k[l], page_table), k], axis=1)
        v_all = jnp.concatenate([gather_pages(cache_dsa_v[l], page_table), v], axis=1)
        ik_all = jnp.concatenate([gather_pages(cache_idx_k[l], page_table), ik], axis=1)
        fk_all = jnp.concatenate([gather_pages(cache_fox_k[l], page_table), fk], axis=1)
        fv_all = jnp.concatenate([gather_pages(cache_fox_v[l], page_table), fv], axis=1)
        logf_all = jnp.concatenate([gather_pages(cache_fox_logf[l], page_table).astype(jnp.float32), logf], axis=1)
        cum_all = jnp.cumsum(logf_all, axis=1)
        o_dsa = dsa_attend(q, iq, iw, pos_s, k_all, v_all, ik_all, pos_all, topk_s)
        o_fox = fox_attend(fq, fk_all, fv_all, cum_all[:, past_len:], cum_all, pos_s, pos_all)
        mix_s = jnp.concatenate([o_dsa, o_fox], axis=-1)
        hs = hs + jnp.einsum('bsm,md->bsd', mix_s, w_o[l])
        hs = hs + squared_relu_mlp(rmsnorm(hs, mlp_norm[l]), w_up[l], w_down[l])
        s_dsa_k.append(k); s_dsa_v.append(v); s_idx_k.append(ik)
        s_fox_k.append(fk); s_fox_v.append(fv); s_fox_logf.append(logf)

    y_prompt = rmsnorm(hp, final_norm)
    y_sample = rmsnorm(hs, final_norm)
    return (y_prompt, y_sample,
            jnp.stack(p_dsa_k), jnp.stack(p_dsa_v), jnp.stack(p_idx_k),
            jnp.stack(p_fox_k), jnp.stack(p_fox_v), jnp.stack(p_fox_logf),
            jnp.stack(s_dsa_k), jnp.stack(s_dsa_v), jnp.stack(s_idx_k),
            jnp.stack(s_fox_k), jnp.stack(s_fox_v), jnp.stack(s_fox_logf))
```

```python
import functools

import jax
import jax.numpy as jnp
import numpy as np
from jax import lax
from jax.experimental import pallas as pl
from jax.experimental.pallas import tpu as pltpu

F32, BF16, I32 = jnp.float32, jnp.bfloat16, jnp.int32

HEAD_DIM = 64
N_DSA_HEADS = 8
N_DSA_KV = 2
DSA_GROUP = N_DSA_HEADS // N_DSA_KV
N_FOX = 8
N_IDX = 4
IDX_DIM = 64
ROPE_THETA = 500000.0
ROPE_DIMS = HEAD_DIM // 4
MAX_TOPK = 256
EPS = 1e-6

LANES = 128
SUBLANES = 8
VMEM_LIMIT = 56 * 1024 * 1024
NEG = -1e30
INT_MIN = -(2 ** 31)
IDX_BIG = 2 ** 30

NT = (((1,), (1,)), ((), ()))
TN = (((0,), (0,)), ((), ()))

T_Q, T_K, T_IQ, T_IK, T_V, T_FQ, T_FK, T_FV, T_SM, T_IW, N_TILES = 0, 4, 5, 7, 8, 9, 13, 17, 21, 22, 26
SM_IW0, SM_FG0 = 0, N_IDX


def _cparams(n_grid):
    return pltpu.CompilerParams(dimension_semantics=("arbitrary",) * n_grid, vmem_limit_bytes=VMEM_LIMIT)


def _split3(x):
    hi = x.astype(BF16)
    r1 = x - hi.astype(F32)
    mid = r1.astype(BF16)
    lo = (r1 - mid.astype(F32)).astype(BF16)
    return hi, mid, lo


def _sortable(x):
    x = jnp.where(x == 0.0, 0.0, x)
    b = lax.bitcast_convert_type(x, I32)
    return b ^ ((b >> 31) & 0x7FFFFFFF)


def _softmax_step(u, v, m_ref, l_ref, acc_ref, idx, v_is_transposed=False):
    m_old = m_ref[idx]
    m_new = jnp.maximum(m_old, jnp.max(u, axis=1, keepdims=True))
    alpha = jnp.exp(m_old - m_new)
    p = jnp.exp(u - m_new)
    l_ref[idx] = alpha * l_ref[idx] + jnp.sum(p, axis=1, keepdims=True)
    if v_is_transposed:
        pv = lax.dot_general(p.astype(BF16), v, NT, preferred_element_type=F32)
    else:
        pv = jnp.dot(p.astype(BF16), v, preferred_element_type=F32)
    acc_ref[idx] = alpha * acc_ref[idx] + pv
    m_ref[idx] = m_new


_PROJ_OUTS = (("q", 512, BF16), ("k", 128, F32), ("kb", 128, BF16), ("v", 128, F32), ("vb", 128, BF16),
              ("iq", 256, BF16), ("ik", 64, F32), ("ik2", 128, BF16), ("iw", 512, F32),
              ("fq", 512, BF16), ("fk", 512, F32), ("fkb", 512, BF16), ("fv", 512, F32), ("fvb", 512, BF16),
              ("sm", 128, F32), ("cum", 128, F32))


def _proj_kernel(x_ref, g_ref, cos_ref, sa_ref, sb_ref, w_ref, bias_ref, *rest, with_cum):
    n_out = len(_PROJ_OUTS) if with_cum else len(_PROJ_OUTS) - 1
    o = dict(zip([n for n, _, _ in _PROJ_OUTS], rest[:n_out]))
    carry_ref = rest[n_out]
    tm = x_ref.shape[0]

    x = x_ref[...]
    ms = jnp.mean(x * x, axis=-1, keepdims=True)
    hn = (x * lax.rsqrt(ms + EPS) * g_ref[...]).astype(BF16)
    y = jnp.dot(hn, w_ref[...], preferred_element_type=F32)

    cos, sa, sb = cos_ref[...], sa_ref[...], sb_ref[...]

    def tile(c):
        return y[:, c * LANES:(c + 1) * LANES]

    def rope(t):
        return t * cos + pltpu.roll(t, LANES - ROPE_DIMS // 2, 1) * sa + pltpu.roll(t, ROPE_DIMS // 2, 1) * sb

    scale = HEAD_DIM ** -0.5
    for c in range(4):
        o["q"][:, c * LANES:(c + 1) * LANES] = (rope(tile(T_Q + c)) * scale).astype(BF16)
    kr = rope(tile(T_K))
    o["k"][...] = kr
    o["kb"][...] = kr.astype(BF16)
    for c in range(2):
        o["iq"][:, c * LANES:(c + 1) * LANES] = rope(tile(T_IQ + c)).astype(BF16)
    ikr = rope(tile(T_IK))
    o["ik"][...] = ikr[:, :IDX_DIM]
    o["ik2"][...] = ikr.astype(BF16)
    vv = tile(T_V)
    o["v"][...] = vv
    o["vb"][...] = vv.astype(BF16)
    for c in range(4):
        sl = slice(c * LANES, (c + 1) * LANES)
        o["fq"][:, sl] = (tile(T_FQ + c) * scale).astype(BF16)
        fk = tile(T_FK + c)
        o["fk"][:, sl] = fk
        o["fkb"][:, sl] = fk.astype(BF16)
        fv = tile(T_FV + c)
        o["fv"][:, sl] = fv
        o["fvb"][:, sl] = fv.astype(BF16)
        o["iw"][:, sl] = tile(T_IW + c) * (N_IDX ** -0.5)

    s = tile(T_SM)
    lane = lax.broadcasted_iota(I32, (tm, LANES), 1)
    z = s + bias_ref[...]
    logsig = jnp.minimum(z, 0.0) - jnp.log1p(jnp.exp(-jnp.abs(z)))
    sm = jnp.where(lane < SM_FG0, s * (N_IDX ** -0.5), logsig)
    o["sm"][...] = sm

    if with_cum:
        @pl.when(pl.program_id(0) == 0)
        def _():
            carry_ref[...] = jnp.zeros(carry_ref.shape, F32)

        parts = jnp.concatenate(_split3(sm), axis=1)
        r = lax.broadcasted_iota(I32, (tm, tm), 0)
        c = lax.broadcasted_iota(I32, (tm, tm), 1)
        tri = jnp.where(c <= r, 1.0, 0.0).astype(BF16)
        cs = jnp.dot(tri, parts, preferred_element_type=F32)
        cum = cs[:, :LANES] + cs[:, LANES:2 * LANES] + cs[:, 2 * LANES:] + carry_ref[0:1, :]
        o["cum"][...] = cum
        carry_ref[0:1, :] = cum[tm - 1:tm, :]


def _project(x2d, gain, tabs, w_all, bias_row, *, with_cum, tm):
    n, d = x2d.shape
    defs = _PROJ_OUTS if with_cum else _PROJ_OUTS[:-1]
    row = lambda w: pl.BlockSpec((tm, w), lambda i: (i, 0))
    const = lambda shape: pl.BlockSpec(shape, lambda i: (0, 0))
    outs = pl.pallas_call(
        functools.partial(_proj_kernel, with_cum=with_cum),
        grid=(n // tm,),
        in_specs=[row(d), const((1, d)), row(LANES), row(LANES), row(LANES), const(w_all.shape), const((1, LANES))],
        out_specs=[row(w) for _, w, _ in defs],
        out_shape=[jax.ShapeDtypeStruct((n, w), dt) for _, w, dt in defs],
        scratch_shapes=[pltpu.VMEM((SUBLANES, LANES), F32)],
        compiler_params=_cparams(1),
        name="proj",
    )(x2d, gain, *tabs, w_all, bias_row)
    return dict(zip([nm for nm, _, _ in defs], outs))


def _fox_prompt_kernel(q_ref, k_ref, v_ref, c_ref, o_ref, m_scr, l_scr, acc_scr, *, tq):
    i = pl.program_id(1)
    lane = lax.broadcasted_iota(I32, (tq, LANES), 1)
    lo = lane < HEAD_DIM
    q = q_ref[...]
    zero = jnp.zeros_like(q)
    qs = (jnp.where(lo, q, zero), jnp.where(lo, zero, q))
    m_scr[...] = jnp.full(m_scr.shape, NEG, F32)
    l_scr[...] = jnp.zeros(l_scr.shape, F32)
    acc_scr[...] = jnp.zeros(acc_scr.shape, F32)

    def step(j, masked):
        off = pl.multiple_of(j * tq, tq)
        k = k_ref[pl.ds(off, tq), :]
        v = v_ref[pl.ds(off, tq), :]
        for a in range(2):
            s = lax.dot_general(qs[a], k, NT, preferred_element_type=F32)
            u = s - c_ref[a:a + 1, pl.ds(off, tq)]
            if masked:
                r = lax.broadcasted_iota(I32, (tq, tq), 0)
                c = lax.broadcasted_iota(I32, (tq, tq), 1)
                u = jnp.where(c <= r, u, NEG)
            _softmax_step(u, v, m_scr, l_scr, acc_scr, a)

    def body(j, carry):
        step(j, False)
        return carry

    lax.fori_loop(0, i, body, 0)
    step(i, True)
    o_ref[...] = jnp.where(lo, acc_scr[0] / l_scr[0], acc_scr[1] / l_scr[1])


def _fox_prompt(fq, fkb, fvb, cum_t, *, tq):
    s = fq.shape[0]
    n_pairs = N_FOX // 2
    return pl.pallas_call(
        functools.partial(_fox_prompt_kernel, tq=tq),
        grid=(n_pairs, s // tq),
        in_specs=[pl.BlockSpec((tq, LANES), lambda h, i: (i, h)),
                  pl.BlockSpec((s, LANES), lambda h, i: (0, h)),
                  pl.BlockSpec((s, LANES), lambda h, i: (0, h)),
                  pl.BlockSpec((None, 2, s), lambda h, i: (h, 0, 0))],
        out_specs=pl.BlockSpec((tq, LANES), lambda h, i: (i, h)),
        out_shape=jax.ShapeDtypeStruct((s, N_FOX * HEAD_DIM), F32),
        scratch_shapes=[pltpu.VMEM((2, tq, 1), F32), pltpu.VMEM((2, tq, 1), F32), pltpu.VMEM((2, tq, LANES), F32)],
        compiler_params=_cparams(2),
        name="fox_prompt",
    )(fq, fkb, fvb, cum_t)


def _select_topk(keys_scr, n_chunks, chunk, rows, topk, nbits):
    tiles = chunk // LANES
    lane = lax.broadcasted_iota(I32, (rows, LANES), 1)

    def count(pred):
        def body(c, acc):
            off = pl.multiple_of(c * chunk, chunk)
            for t in range(tiles):
                acc = acc + jnp.where(pred(keys_scr[:, pl.ds(off + t * LANES, LANES)]), 1, 0)
            return acc
        acc = lax.fori_loop(0, n_chunks, body, jnp.zeros((rows, LANES), I32))
        return jnp.sum(acc.astype(F32), axis=1, keepdims=True)

    kf = float(topk)
    t0 = jnp.where(count(lambda b: b >= 0) >= kf, 0, INT_MIN).astype(I32)

    def bit_body(s, t):
        cand = t | lax.shift_left(jnp.int32(1), 30 - s)
        cand_b = jnp.broadcast_to(cand, (rows, LANES))
        return jnp.where(count(lambda b: b >= cand_b) >= kf, cand, t)

    thr = lax.fori_loop(0, 31, bit_body, t0)
    thr_b = jnp.broadcast_to(jnp.maximum(thr, INT_MIN + 1), (rows, LANES))

    def rank_body(c, carry):
        off = pl.multiple_of(c * chunk, chunk)
        for t in range(tiles):
            sl = pl.ds(off + t * LANES, LANES)
            blk = keys_scr[:, sl]
            pos = off + t * LANES + lane
            keys_scr[:, sl] = jnp.where(blk > thr_b, -1, jnp.where(blk == thr_b, pos, IDX_BIG))
        return carry

    lax.fori_loop(0, n_chunks, rank_body, 0)

    def idx_body(s, m):
        cand = m | lax.shift_left(jnp.int32(1), nbits - 1 - s)
        cand_b = jnp.broadcast_to(cand, (rows, LANES))
        return jnp.where(count(lambda b: b < cand_b) < kf, cand, m)

    return lax.fori_loop(0, nbits, idx_body, jnp.zeros((rows, 1), I32))


def _dsa_prompt_kernel(iq_ref, iw_ref, q_ref, ik_ref, k_ref, v_ref, o_ref,
                       keys_scr, qst_scr, m_scr, l_scr, acc_scr, *, tq, kc, topk, nbits):
    i = pl.program_id(0)
    n_chunks = (i * tq) // kc + 1
    tiles = kc // LANES
    lane = lax.broadcasted_iota(I32, (tq, LANES), 1)
    lo = lane < HEAD_DIM
    qpos = i * tq + lax.broadcasted_iota(I32, (tq, LANES), 0)

    iq = iq_ref[...]
    zero = jnp.zeros((tq, LANES), BF16)
    iqm = []
    for h in range(N_IDX):
        t = iq[:, (h // 2) * LANES:(h // 2 + 1) * LANES]
        iqm.append(jnp.where(lo, t, zero) if h % 2 == 0 else jnp.where(lo, zero, t))

    def score_body(c, carry):
        off = pl.multiple_of(c * kc, kc)
        ik = ik_ref[pl.ds(off, kc), :]
        xs = [lax.dot_general(iqm[h], ik, NT, preferred_element_type=F32) for h in range(N_IDX)]
        for t in range(tiles):
            sc = jnp.zeros((tq, LANES), F32)
            for h in range(N_IDX):
                sc = sc + jnp.maximum(xs[h][:, t * LANES:(t + 1) * LANES], 0.0) * iw_ref[:, h * LANES:(h + 1) * LANES]
            kpos = off + t * LANES + lane
            keys_scr[:, pl.ds(off + t * LANES, LANES)] = jnp.where(kpos <= qpos, _sortable(sc), INT_MIN)
        return carry

    lax.fori_loop(0, n_chunks, score_body, 0)
    m_sel = _select_topk(keys_scr, n_chunks, kc, tq, topk, nbits)
    m_b = jnp.broadcast_to(m_sel, (tq, LANES))

    q = q_ref[...]
    for n in range(N_DSA_KV):
        for g in range(DSA_GROUP):
            t = q[:, g * LANES:(g + 1) * LANES]
            qst_scr[n, g * tq:(g + 1) * tq, :] = jnp.where(lo, t, zero) if n == 0 else jnp.where(lo, zero, t)
    m_scr[...] = jnp.full(m_scr.shape, NEG, F32)
    l_scr[...] = jnp.zeros(l_scr.shape, F32)
    acc_scr[...] = jnp.zeros(acc_scr.shape, F32)

    def att_body(c, carry):
        off = pl.multiple_of(c * kc, kc)
        k = k_ref[pl.ds(off, kc), :]
        v = v_ref[pl.ds(off, kc), :]
        bias = jnp.concatenate(
            [jnp.where(keys_scr[:, pl.ds(off + t * LANES, LANES)] <= m_b, 0.0, NEG) for t in range(tiles)], axis=1)
        bias = jnp.concatenate([bias] * DSA_GROUP, axis=0)
        for n in range(N_DSA_KV):
            u = lax.dot_general(qst_scr[n], k, NT, preferred_element_type=F32) + bias
            _softmax_step(u, v, m_scr, l_scr, acc_scr, n)
        return carry

    lax.fori_loop(0, n_chunks, att_body, 0)
    for g in range(DSA_GROUP):
        rows = slice(g * tq, (g + 1) * tq)
        o_ref[:, g * LANES:(g + 1) * LANES] = jnp.where(
            lo, acc_scr[0, rows, :] / l_scr[0, rows, :], acc_scr[1, rows, :] / l_scr[1, rows, :])


def _dsa_prompt(iq, iw, q, ik2, kb, vb, *, tq, kc, topk):
    s = q.shape[0]
    row = lambda w: pl.BlockSpec((tq, w), lambda i: (i, 0))
    full = lambda: pl.BlockSpec((s, LANES), lambda i: (0, 0))
    return pl.pallas_call(
        functools.partial(_dsa_prompt_kernel, tq=tq, kc=kc, topk=topk, nbits=max(1, (s - 1).bit_length())),
        grid=(s // tq,),
        in_specs=[row(N_IDX * IDX_DIM), row(N_IDX * LANES), row(N_DSA_HEADS * HEAD_DIM), full(), full(), full()],
        out_specs=row(N_DSA_HEADS * HEAD_DIM),
        out_shape=jax.ShapeDtypeStruct((s, N_DSA_HEADS * HEAD_DIM), F32),
        scratch_shapes=[pltpu.VMEM((tq, s), I32),
                        pltpu.VMEM((N_DSA_KV, DSA_GROUP * tq, LANES), BF16),
                        pltpu.VMEM((N_DSA_KV, DSA_GROUP * tq, 1), F32),
                        pltpu.VMEM((N_DSA_KV, DSA_GROUP * tq, 1), F32),
                        pltpu.VMEM((N_DSA_KV, DSA_GROUP * tq, LANES), F32)],
        compiler_params=_cparams(1),
        name="dsa_prompt",
    )(iq, iw, q, ik2, kb, vb)


def _mlp_kernel(x_ref, dsa_ref, fox_ref, wod_ref, wof_ref, g_ref, wu_ref, wd_ref, gf_ref, o_ref, *, final):
    h = (x_ref[...]
         + jnp.dot(dsa_ref[...].astype(BF16), wod_ref[...], preferred_element_type=F32)
         + jnp.dot(fox_ref[...].astype(BF16), wof_ref[...], preferred_element_type=F32))
    ms = jnp.mean(h * h, axis=-1, keepdims=True)
    hn = (h * lax.rsqrt(ms + EPS) * g_ref[...]).astype(BF16)
    a = jnp.maximum(jnp.dot(hn, wu_ref[...], preferred_element_type=F32), 0.0)
    out = h + jnp.dot((a * a).astype(BF16), wd_ref[...], preferred_element_type=F32)
    if final:
        ms = jnp.mean(out * out, axis=-1, keepdims=True)
        out = out * lax.rsqrt(ms + EPS) * gf_ref[...]
    o_ref[...] = out


def _mlp(x2d, dsa, fox, wod, wof, gain, wu, wd, gain_f, *, final, tm):
    n, d = x2d.shape
    row = lambda w: pl.BlockSpec((tm, w), lambda i: (i, 0))
    const = lambda a: pl.BlockSpec(a.shape, lambda i: (0, 0), pipeline_mode=pl.Buffered(1))
    return pl.pallas_call(
        functools.partial(_mlp_kernel, final=final),
        grid=(n // tm,),
        in_specs=[row(d), row(dsa.shape[1]), row(fox.shape[1]), const(wod), const(wof), const(gain),
                  const(wu), const(wd), const(gain_f)],
        out_specs=row(d),
        out_shape=jax.ShapeDtypeStruct((n, d), F32),
        compiler_params=_cparams(1),
        name="mlp",
    )(x2d, dsa, fox, wod, wof, gain, wu, wd, gain_f)


def _sample_fox_kernel(pt_ref, qf_ref, fkn_ref, fvn_ref, lnt_ref, iqs_ref, iws_ref, *rest, pages, n_new):
    ik_refs = rest[0:pages]
    fk_refs = rest[pages:2 * pages]
    fv_refs = rest[2 * pages:3 * pages]
    lf_refs = rest[3 * pages:4 * pages]
    o_ref, sc_ref, m_scr, l_scr, acc_scr, carry_scr = rest[4 * pages:]
    j = pl.program_id(1)
    page = ik_refs[0].shape[1]

    @pl.when(j == 0)
    def _():
        m_scr[...] = jnp.full(m_scr.shape, NEG, F32)
        l_scr[...] = jnp.zeros(l_scr.shape, F32)
        acc_scr[...] = jnp.zeros(acc_scr.shape, F32)
        carry_scr[...] = jnp.zeros(carry_scr.shape, F32)

    r_i = lax.broadcasted_iota(I32, (page, 2 * page), 0)
    c_i = lax.broadcasted_iota(I32, (page, 2 * page), 1)
    scan_w = jnp.where(r_i <= c_i, 1.0, 0.0).astype(BF16)

    def lane_cumsum(x):
        out = None
        for p in _split3(x):
            t = jnp.dot(p, scan_w, preferred_element_type=F32)
            out = t if out is None else out + t
        return out

    for r in range(pages):
        rel = jnp.dot(iqs_ref[...], ik_refs[r][...].astype(BF16), preferred_element_type=F32)
        rel = jnp.maximum(rel, 0.0) * iws_ref[...]
        sc = rel[0:SUBLANES]
        for h in range(1, N_IDX):
            sc = sc + rel[h * SUBLANES:(h + 1) * SUBLANES]
        sc_ref[:, r * page:(r + 1) * page] = sc

        cs = lane_cumsum(lf_refs[r][...])
        cum = carry_scr[...] + cs[:, :page]
        carry_scr[...] = carry_scr[...] + cs[:, page:]

        for h in range(N_FOX):
            u = jnp.dot(qf_ref[h], fk_refs[r][h].astype(BF16), preferred_element_type=F32) - cum[h:h + 1, :]
            _softmax_step(u, fv_refs[r][h].astype(BF16), m_scr, l_scr, acc_scr, h, v_is_transposed=True)

    @pl.when(j == pl.num_programs(1) - 1)
    def _():
        cum = carry_scr[...] + lane_cumsum(lnt_ref[...])[:, :page]
        row = lax.broadcasted_iota(I32, (SUBLANES, page), 0)
        col = lax.broadcasted_iota(I32, (SUBLANES, page), 1)
        valid = (col < n_new) & (col <= row)
        for h in range(N_FOX):
            u = jnp.dot(qf_ref[h], fkn_ref[h], preferred_element_type=F32) - cum[h:h + 1, :]
            _softmax_step(jnp.where(valid, u, NEG), fvn_ref[h], m_scr, l_scr, acc_scr, h, v_is_transposed=True)
        for h in range(N_FOX):
            o_ref[h] = acc_scr[h] / l_scr[h]


def _sample_fox(page_table, qf, fkn, fvn, lnt, iqs, iws, c_ik, c_fk, c_fv, c_lf, layer, *, pages, n_new):
    nb, n_pages = page_table.shape
    page = c_ik.shape[3]
    per_b = lambda shape: pl.BlockSpec((None,) + shape, lambda b, j, pt: (b,) + (0,) * len(shape))

    def paged(arr, r):
        blk = arr.shape[2:]
        return pl.BlockSpec((None, None) + blk,
                            lambda b, j, pt: (layer, pt[b, j * pages + r]) + (0,) * len(blk))

    in_specs = [per_b(qf.shape[1:]), per_b(fkn.shape[1:]), per_b(fvn.shape[1:]), per_b(lnt.shape[1:]),
                per_b(iqs.shape[1:]), per_b(iws.shape[1:])]
    operands = [qf, fkn, fvn, lnt, iqs, iws]
    for arr in (c_ik, c_fk, c_fv, c_lf):
        for r in range(pages):
            in_specs.append(paged(arr, r))
            operands.append(arr)
    grid_spec = pltpu.PrefetchScalarGridSpec(
        num_scalar_prefetch=1,
        grid=(nb, n_pages // pages),
        in_specs=in_specs,
        out_specs=[pl.BlockSpec((None, N_FOX, SUBLANES, HEAD_DIM), lambda b, j, pt: (b, 0, 0, 0)),
                   pl.BlockSpec((None, SUBLANES, pages * page), lambda b, j, pt: (b, 0, j))],
        scratch_shapes=[pltpu.VMEM((N_FOX, SUBLANES, 1), F32), pltpu.VMEM((N_FOX, SUBLANES, 1), F32),
                        pltpu.VMEM((N_FOX, SUBLANES, HEAD_DIM), F32), pltpu.VMEM((N_FOX, page), F32)],
    )
    return pl.pallas_call(
        functools.partial(_sample_fox_kernel, pages=pages, n_new=n_new),
        grid_spec=grid_spec,
        out_shape=[jax.ShapeDtypeStruct((nb, N_FOX, SUBLANES, HEAD_DIM), F32),
                   jax.ShapeDtypeStruct((nb, SUBLANES, n_pages * page), F32)],
        compiler_params=_cparams(2),
        name="sample_fox",
    )(page_table, *operands)


def _sample_dsa_kernel(pt_ref, sc_ref, iqs_ref, iws_ref, ikn_ref, qd_ref, kdn_ref, vdn_ref, *rest,
                       pages, n_new, topk, chunk, nbits):
    k_refs = rest[0:pages]
    v_refs = rest[pages:2 * pages]
    o_ref, keys_scr, bias_scr, m_scr, l_scr, acc_scr = rest[2 * pages:]
    j = pl.program_id(1)
    page = ikn_ref.shape[1]
    n_past = sc_ref.shape[1]

    @pl.when(j == 0)
    def _():
        m_scr[...] = jnp.full(m_scr.shape, NEG, F32)
        l_scr[...] = jnp.zeros(l_scr.shape, F32)
        acc_scr[...] = jnp.zeros(acc_scr.shape, F32)
        for c in range(n_past // LANES):
            keys_scr[:, c * LANES:(c + 1) * LANES] = _sortable(sc_ref[:, c * LANES:(c + 1) * LANES])
        rel = jnp.maximum(jnp.dot(iqs_ref[...], ikn_ref[...], preferred_element_type=F32), 0.0) * iws_ref[...]
        sc = rel[0:SUBLANES]
        for h in range(1, N_IDX):
            sc = sc + rel[h * SUBLANES:(h + 1) * SUBLANES]
        row = lax.broadcasted_iota(I32, (SUBLANES, page), 0)
        col = lax.broadcasted_iota(I32, (SUBLANES, page), 1)
        keys_scr[:, n_past:n_past + page] = jnp.where((col < n_new) & (col <= row), _sortable(sc), INT_MIN)
        n_total = n_past + page
        n_pad = keys_scr.shape[1]
        for c in range(n_total // LANES, n_pad // LANES):
            keys_scr[:, c * LANES:(c + 1) * LANES] = jnp.full((SUBLANES, LANES), INT_MIN, I32)
        m_sel = _select_topk(keys_scr, n_pad // chunk, chunk, SUBLANES, topk, nbits)
        m_b = jnp.broadcast_to(m_sel, (SUBLANES, LANES))
        for c in range(n_total // LANES):
            sl = slice(c * LANES, (c + 1) * LANES)
            bias_scr[:, sl] = jnp.where(keys_scr[:, sl] <= m_b, 0.0, NEG)

    def attend(n, kn, vn, bias):
        u = jnp.dot(qd_ref[n], kn, preferred_element_type=F32) + jnp.concatenate([bias] * DSA_GROUP, axis=0)
        _softmax_step(u, vn, m_scr, l_scr, acc_scr, n, v_is_transposed=True)

    for r in range(pages):
        off = pl.multiple_of((j * pages + r) * page, page)
        bias = bias_scr[:, pl.ds(off, page)]
        for n in range(N_DSA_KV):
            attend(n, k_refs[r][n].astype(BF16), v_refs[r][n].astype(BF16), bias)

    @pl.when(j == pl.num_programs(1) - 1)
    def _():
        bias = bias_scr[:, n_past:n_past + page]
        for n in range(N_DSA_KV):
            attend(n, kdn_ref[n], vdn_ref[n], bias)
        for n in range(N_DSA_KV):
            o_ref[n] = acc_scr[n] / l_scr[n]


def _sample_dsa(page_table, scores, iqs, iws, ikn, qd, kdn, vdn, c_k, c_v, layer, *, pages, n_new, topk):
    nb, n_pages = page_table.shape
    page = ikn.shape[2]
    n_past = scores.shape[2]
    n_total = n_past + page
    chunk = 2048
    n_pad = -(-n_total // chunk) * chunk
    rows_q = DSA_GROUP * SUBLANES
    per_b = lambda shape: pl.BlockSpec((None,) + shape, lambda b, j, pt: (b,) + (0,) * len(shape))

    def paged(arr, r):
        blk = arr.shape[2:]
        return pl.BlockSpec((None, None) + blk,
                            lambda b, j, pt: (layer, pt[b, j * pages + r]) + (0,) * len(blk))

    in_specs = [per_b(a.shape[1:]) for a in (scores, iqs, iws, ikn, qd, kdn, vdn)]
    operands = [scores, iqs, iws, ikn, qd, kdn, vdn]
    for arr in (c_k, c_v):
        for r in range(pages):
            in_specs.append(paged(arr, r))
            operands.append(arr)
    grid_spec = pltpu.PrefetchScalarGridSpec(
        num_scalar_prefetch=1,
        grid=(nb, n_pages // pages),
        in_specs=in_specs,
        out_specs=pl.BlockSpec((None, N_DSA_KV, rows_q, HEAD_DIM), lambda b, j, pt: (b, 0, 0, 0)),
        scratch_shapes=[pltpu.VMEM((SUBLANES, n_pad), I32), pltpu.VMEM((SUBLANES, n_total), F32),
                        pltpu.VMEM((N_DSA_KV, rows_q, 1), F32), pltpu.VMEM((N_DSA_KV, rows_q, 1), F32),
                        pltpu.VMEM((N_DSA_KV, rows_q, HEAD_DIM), F32)],
    )
    return pl.pallas_call(
        functools.partial(_sample_dsa_kernel, pages=pages, n_new=n_new, topk=topk, chunk=chunk,
                          nbits=max(1, (n_total - 1).bit_length())),
        grid_spec=grid_spec,
        out_shape=jax.ShapeDtypeStruct((nb, N_DSA_KV, rows_q, HEAD_DIM), F32),
        compiler_params=_cparams(2),
        name="sample_dsa",
    )(page_table, *operands)


def _rope_tables(pos):
    half = ROPE_DIMS // 2
    inv_freq = jnp.power(ROPE_THETA, -jnp.arange(half, dtype=F32) * 2.0 / ROPE_DIMS)
    ang = pos.astype(F32)[:, None] * inv_freq[None, :]
    cos, sin = jnp.cos(ang), jnp.sin(ang)
    n = pos.shape[0]
    ones = jnp.ones((n, HEAD_DIM - ROPE_DIMS), F32)
    zeros = jnp.zeros((n, HEAD_DIM - ROPE_DIMS), F32)
    zh = jnp.zeros((n, half), F32)
    rep = LANES // HEAD_DIM
    cos_t = jnp.tile(jnp.concatenate([cos, cos, ones], axis=1), (1, rep))
    sa_t = jnp.tile(jnp.concatenate([-sin, zh, zeros], axis=1), (1, rep))
    sb_t = jnp.tile(jnp.concatenate([zh, sin, zeros], axis=1), (1, rep))
    return cos_t, sa_t, sb_t


def _layer_weights(w_in_l, b_f_l, w_o_l):
    d = w_in_l.shape[0]
    dq, dkv, iq, fw = N_DSA_HEADS * HEAD_DIM, N_DSA_KV * HEAD_DIM, N_IDX * IDX_DIM, N_FOX * HEAD_DIM
    cuts = np.cumsum([dq, dkv, dkv, iq, IDX_DIM, N_IDX, fw, fw, fw, N_FOX])[:-1].tolist()
    wq, wk, wv, wiq, wik, wiw, wfq, wfk, wfv, wfg = jnp.split(w_in_l, cuts, axis=1)
    wq = wq.reshape(d, N_DSA_KV, DSA_GROUP, HEAD_DIM).transpose(0, 2, 1, 3).reshape(d, dq)
    small = jnp.concatenate([wiw, wfg, jnp.zeros((d, LANES - N_IDX - N_FOX), w_in_l.dtype)], axis=1)
    iw_rep = jnp.repeat(wiw, LANES, axis=1)
    w_all = jnp.concatenate([wq, wk, wiq, wik, wik, wv, wfq, wfk, wfv, small, iw_rep], axis=1).astype(BF16)
    assert w_all.shape[1] == N_TILES * LANES
    bias_row = jnp.zeros((1, LANES), F32).at[0, SM_FG0:SM_FG0 + N_FOX].set(b_f_l.astype(F32))
    wod = w_o_l[:dq].reshape(N_DSA_KV, DSA_GROUP, HEAD_DIM, -1).transpose(1, 0, 2, 3).reshape(dq, -1).astype(BF16)
    wof = w_o_l[dq:].astype(BF16)
    return w_all, bias_row, wod, wof


def _pad_axis(x, axis, size):
    pad = [(0, 0)] * x.ndim
    pad[axis] = (0, size - x.shape[axis])
    return jnp.pad(x, pad)


def kernel(x_prompt, x_sample, cache_dsa_k, cache_dsa_v, cache_idx_k, cache_fox_k, cache_fox_v, cache_fox_logf,
           page_table, attn_norm, w_in, b_f, w_o, mlp_norm, w_up, w_down, final_norm):
    B, S, D = x_prompt.shape
    NB, T, _ = x_sample.shape
    depth, n_pool, page = cache_dsa_k.shape[0], cache_dsa_k.shape[1], cache_dsa_k.shape[2]
    n_pages = page_table.shape[1]
    past_len = n_pages * page
    assert B == 1 and page == LANES and T <= SUBLANES
    topk_p = min(MAX_TOPK, S // 4)
    topk_s = min(MAX_TOPK, (past_len + T) // 4)

    tm_p = min(256, S)
    tq_fox = min(256, S)
    tq_dsa = min(128, S)
    kc_dsa = min(512, S)
    pages_per_step = 4 if n_pages % 4 == 0 else 1
    ns = NB * T

    tabs_p = _rope_tables(jnp.arange(S, dtype=I32))
    tabs_s = _rope_tables(past_len + jnp.tile(jnp.arange(T, dtype=I32), NB))

    c_dk = cache_dsa_k.transpose(0, 1, 3, 4, 2)
    c_dv = cache_dsa_v.transpose(0, 1, 3, 4, 2)
    c_fk = cache_fox_k.transpose(0, 1, 3, 4, 2)
    c_fv = cache_fox_v.transpose(0, 1, 3, 4, 2)
    c_ik = cache_idx_k.transpose(0, 1, 3, 2)
    c_lf = cache_fox_logf.transpose(0, 1, 3, 2)

    hp = x_prompt.reshape(S, D)
    hs = x_sample.reshape(ns, D)
    gain_f = final_norm.reshape(1, D).astype(F32)
    outs_p = {k: [] for k in ("k", "v", "ik", "fk", "fv", "lf")}
    outs_s = {k: [] for k in ("k", "v", "ik", "fk", "fv", "lf")}

    for l in range(depth):
        w_all, bias_row, wod, wof = _layer_weights(w_in[l], b_f[l], w_o[l])
        g_attn = attn_norm[l].reshape(1, D).astype(F32)
        g_mlp = mlp_norm[l].reshape(1, D).astype(F32)
        wu = w_up[l].astype(BF16)
        wd = w_down[l].astype(BF16)
        final = l == depth - 1

        pr = _project(hp, g_attn, tabs_p, w_all, bias_row, with_cum=True, tm=tm_p)
        cum_t = pr["cum"][:, SM_FG0:SM_FG0 + N_FOX].T.reshape(N_FOX // 2, 2, S)
        o_fox = _fox_prompt(pr["fq"], pr["fkb"], pr["fvb"], cum_t, tq=tq_fox)
        o_dsa = _dsa_prompt(pr["iq"], pr["iw"], pr["q"], pr["ik2"], pr["kb"], pr["vb"],
                            tq=tq_dsa, kc=kc_dsa, topk=topk_p)
        hp = _mlp(hp, o_dsa, o_fox, wod, wof, g_mlp, wu, wd, gain_f, final=final, tm=tm_p)
        for name, key in (("k", "k"), ("v", "v"), ("ik", "ik"), ("fk", "fk"), ("fv", "fv")):
            outs_p[name].append(pr[key])
        outs_p["lf"].append(pr["sm"][:, SM_FG0:SM_FG0 + N_FOX])

        sr = _project(hs, g_attn, tabs_s, w_all, bias_row, with_cum=False, tm=ns)
        by_head = lambda a, nh: a.reshape(NB, T, nh, -1).transpose(0, 2, 1, 3)
        as_page = lambda a, nh: _pad_axis(a.reshape(NB, T, nh, -1).transpose(0, 2, 3, 1), 3, page)
        qf = _pad_axis(by_head(sr["fq"], N_FOX), 2, SUBLANES)
        fkn = as_page(sr["fkb"], N_FOX)
        fvn = as_page(sr["fvb"], N_FOX)
        lf_new = sr["sm"][:, SM_FG0:SM_FG0 + N_FOX]
        lnt = _pad_axis(lf_new.reshape(NB, T, N_FOX).transpose(0, 2, 1), 2, page)
        iqs = _pad_axis(by_head(sr["iq"], N_IDX), 2, SUBLANES).reshape(NB, N_IDX * SUBLANES, IDX_DIM)
        iws = _pad_axis(by_head(sr["iw"], N_IDX), 2, SUBLANES).reshape(NB, N_IDX * SUBLANES, LANES)
        ikn = as_page(sr["ik2"][:, :IDX_DIM], 1)[:, 0]
        qd = sr["q"].reshape(NB, T, DSA_GROUP, N_DSA_KV, HEAD_DIM).transpose(0, 3, 2, 1, 4)
        qd = _pad_axis(qd, 3, SUBLANES).reshape(NB, N_DSA_KV, DSA_GROUP * SUBLANES, HEAD_DIM)
        kdn = as_page(sr["kb"], N_DSA_KV)
        vdn = as_page(sr["vb"], N_DSA_KV)

        of_s, scores = _sample_fox(page_table, qf, fkn, fvn, lnt, iqs, iws, c_ik, c_fk, c_fv, c_lf, l,
                                   pages=pages_per_step, n_new=T)
        od_s = _sample_dsa(page_table, scores, iqs, iws, ikn, qd, kdn, vdn, c_dk, c_dv, l,
                           pages=pages_per_step, n_new=T, topk=topk_s)
        o_fox_s = of_s[:, :, :T].transpose(0, 2, 1, 3).reshape(ns, N_FOX * HEAD_DIM)
        o_dsa_s = od_s.reshape(NB, N_DSA_KV, DSA_GROUP, SUBLANES, HEAD_DIM)[:, :, :, :T]
        o_dsa_s = o_dsa_s.transpose(0, 3, 2, 1, 4).reshape(ns, N_DSA_HEADS * HEAD_DIM)
        hs = _mlp(hs, o_dsa_s, o_fox_s, wod, wof, g_mlp, wu, wd, gain_f, final=final, tm=ns)
        for name, key in (("k", "k"), ("v", "v"), ("ik", "ik"), ("fk", "fk"), ("fv", "fv")):
            outs_s[name].append(sr[key])
        outs_s["lf"].append(lf_new)

    def pack(outs, lead):
        st = lambda name: jnp.stack(outs[name])
        return (st("k").reshape((depth,) + lead + (N_DSA_KV, HEAD_DIM)),
                st("v").reshape((depth,) + lead + (N_DSA_KV, HEAD_DIM)),
                st("ik").reshape((depth,) + lead + (IDX_DIM,)),
                st("fk").reshape((depth,) + lead + (N_FOX, HEAD_DIM)),
                st("fv").reshape((depth,) + lead + (N_FOX, HEAD_DIM)),
                st("lf").reshape((depth,) + lead + (N_FOX,)))

    return (hp.reshape(B, S, D), hs.reshape(NB, T, D)) + pack(outs_p, (B, S)) + pack(outs_s, (NB, T))
```

```python
import functools

import jax
import jax.numpy as jnp
import numpy as np
from jax import lax
from jax.experimental import pallas as pl
from jax.experimental.pallas import tpu as pltpu

F32, BF16, I32 = jnp.float32, jnp.bfloat16, jnp.int32

HEAD_DIM = 64
N_DSA_HEADS = 8
N_DSA_KV = 2
DSA_GROUP = N_DSA_HEADS // N_DSA_KV
N_FOX = 8
N_IDX = 4
IDX_DIM = 64
ROPE_THETA = 500000.0
ROPE_DIMS = HEAD_DIM // 4
MAX_TOPK = 256
EPS = 1e-6

LANES = 128
SUBLANES = 8
VMEM_LIMIT = 56 * 1024 * 1024
NEG = -1e30
LOG2E = 1.4426950408889634
INT_MIN = -(2 ** 31)
IDX_BIG = 2 ** 30

NT = (((1,), (1,)), ((), ()))

T_Q, T_K, T_IQ, T_IK, T_V, T_FQ, T_FK, T_FV, T_SM, T_IW, N_TILES = 0, 4, 5, 7, 8, 9, 13, 17, 21, 22, 26
SM_IW0, SM_FG0 = 0, N_IDX


def _cparams(n_grid):
    return pltpu.CompilerParams(dimension_semantics=("arbitrary",) * n_grid, vmem_limit_bytes=VMEM_LIMIT)


def _split3(x):
    hi = x.astype(BF16)
    r1 = x - hi.astype(F32)
    mid = r1.astype(BF16)
    lo = (r1 - mid.astype(F32)).astype(BF16)
    return hi, mid, lo


def _sortable(x):
    x = jnp.where(x == 0.0, 0.0, x)
    b = lax.bitcast_convert_type(x, I32)
    return b ^ ((b >> 31) & 0x7FFFFFFF)


def _softmax_step(u, v, m_ref, l_ref, acc_ref, idx, v_is_transposed=False):
    m_old = m_ref[idx]
    m_new = jnp.maximum(m_old, jnp.max(u, axis=1, keepdims=True))
    alpha = jnp.exp2(m_old - m_new)
    p = jnp.exp2(u - m_new)
    l_ref[idx] = alpha * l_ref[idx] + jnp.sum(p, axis=1, keepdims=True)
    if v_is_transposed:
        pv = lax.dot_general(p.astype(BF16), v, NT, preferred_element_type=F32)
    else:
        pv = jnp.dot(p.astype(BF16), v, preferred_element_type=F32)
    acc_ref[idx] = alpha * acc_ref[idx] + pv
    m_ref[idx] = m_new


_PROJ_OUTS = (("q", 512, BF16), ("k", 128, F32), ("kb", 128, BF16), ("v", 128, F32), ("vb", 128, BF16),
              ("iq", 256, BF16), ("ik", 64, F32), ("ik2", 128, BF16), ("iw", 512, F32),
              ("fq", 512, BF16), ("fk", 512, F32), ("fkb", 512, BF16), ("fv", 512, F32), ("fvb", 512, BF16),
              ("sm", 128, F32), ("cum", 128, F32))


def _proj_kernel(x_ref, g_ref, cos_ref, sa_ref, sb_ref, w_ref, bias_ref, *rest, with_cum):
    n_out = len(_PROJ_OUTS) if with_cum else len(_PROJ_OUTS) - 1
    o = dict(zip([n for n, _, _ in _PROJ_OUTS], rest[:n_out]))
    carry_ref = rest[n_out]
    tm = x_ref.shape[0]

    x = x_ref[...]
    ms = jnp.mean(x * x, axis=-1, keepdims=True)
    hn = (x * lax.rsqrt(ms + EPS) * g_ref[...]).astype(BF16)
    y = jnp.dot(hn, w_ref[...], preferred_element_type=F32)

    cos, sa, sb = cos_ref[...], sa_ref[...], sb_ref[...]

    def tile(c):
        return y[:, c * LANES:(c + 1) * LANES]

    def rope(t):
        return t * cos + pltpu.roll(t, LANES - ROPE_DIMS // 2, 1) * sa + pltpu.roll(t, ROPE_DIMS // 2, 1) * sb

    scale = HEAD_DIM ** -0.5 * LOG2E
    for c in range(4):
        o["q"][:, c * LANES:(c + 1) * LANES] = (rope(tile(T_Q + c)) * scale).astype(BF16)
    kr = rope(tile(T_K))
    o["k"][...] = kr
    o["kb"][...] = kr.astype(BF16)
    for c in range(2):
        o["iq"][:, c * LANES:(c + 1) * LANES] = rope(tile(T_IQ + c)).astype(BF16)
    ikr = rope(tile(T_IK))
    o["ik"][...] = ikr[:, :IDX_DIM]
    o["ik2"][...] = ikr.astype(BF16)
    vv = tile(T_V)
    o["v"][...] = vv
    o["vb"][...] = vv.astype(BF16)
    for c in range(4):
        sl = slice(c * LANES, (c + 1) * LANES)
        o["fq"][:, sl] = (tile(T_FQ + c) * scale).astype(BF16)
        fk = tile(T_FK + c)
        o["fk"][:, sl] = fk
        o["fkb"][:, sl] = fk.astype(BF16)
        fv = tile(T_FV + c)
        o["fv"][:, sl] = fv
        o["fvb"][:, sl] = fv.astype(BF16)
        o["iw"][:, sl] = tile(T_IW + c) * (N_IDX ** -0.5)

    s = tile(T_SM)
    lane = lax.broadcasted_iota(I32, (tm, LANES), 1)
    z = s + bias_ref[...]
    logsig = jnp.minimum(z, 0.0) - jnp.log1p(jnp.exp(-jnp.abs(z)))
    sm = jnp.where(lane < SM_FG0, s * (N_IDX ** -0.5), logsig)
    o["sm"][...] = sm

    if with_cum:
        @pl.when(pl.program_id(0) == 0)
        def _():
            carry_ref[...] = jnp.zeros(carry_ref.shape, F32)

        parts = jnp.concatenate(_split3(sm), axis=1)
        r = lax.broadcasted_iota(I32, (tm, tm), 0)
        c = lax.broadcasted_iota(I32, (tm, tm), 1)
        tri = jnp.where(c <= r, 1.0, 0.0).astype(BF16)
        cs = jnp.dot(tri, parts, preferred_element_type=F32)
        cum = cs[:, :LANES] + cs[:, LANES:2 * LANES] + cs[:, 2 * LANES:] + carry_ref[0:1, :]
        o["cum"][...] = cum * LOG2E
        carry_ref[0:1, :] = cum[tm - 1:tm, :]


def _project(x2d, gain, tabs, w_all, bias_row, *, with_cum, tm):
    n, d = x2d.shape
    defs = _PROJ_OUTS if with_cum else _PROJ_OUTS[:-1]
    row = lambda w: pl.BlockSpec((tm, w), lambda i: (i, 0))
    const = lambda shape: pl.BlockSpec(shape, lambda i: (0, 0))
    outs = pl.pallas_call(
        functools.partial(_proj_kernel, with_cum=with_cum),
        grid=(n // tm,),
        in_specs=[row(d), const((1, d)), row(LANES), row(LANES), row(LANES), const(w_all.shape), const((1, LANES))],
        out_specs=[row(w) for _, w, _ in defs],
        out_shape=[jax.ShapeDtypeStruct((n, w), dt) for _, w, dt in defs],
        scratch_shapes=[pltpu.VMEM((SUBLANES, LANES), F32)],
        compiler_params=_cparams(1),
        name="proj",
    )(x2d, gain, *tabs, w_all, bias_row)
    return dict(zip([nm for nm, _, _ in defs], outs))


def _fox_prompt_kernel(q_ref, k_ref, v_ref, c_ref, o_ref, m_scr, l_scr, acc_scr, *, tq, tk):
    i = pl.program_id(1)
    lane = lax.broadcasted_iota(I32, (tq, LANES), 1)
    lo = lane < HEAD_DIM
    q = q_ref[...]
    zero = jnp.zeros_like(q)
    qs = (jnp.where(lo, q, zero), jnp.where(lo, zero, q))
    m_scr[...] = jnp.full(m_scr.shape, NEG, F32)
    l_scr[...] = jnp.zeros(l_scr.shape, F32)
    acc_scr[...] = jnp.zeros(acc_scr.shape, F32)

    def step(j, masked):
        off = pl.multiple_of(j * tk, tk)
        k = k_ref[pl.ds(off, tk), :]
        v = v_ref[pl.ds(off, tk), :]
        for a in range(2):
            s = lax.dot_general(qs[a], k, NT, preferred_element_type=F32)
            u = s - c_ref[a:a + 1, pl.ds(off, tk)]
            if masked:
                r = i * tq + lax.broadcasted_iota(I32, (tq, tk), 0)
                c = off + lax.broadcasted_iota(I32, (tq, tk), 1)
                u = jnp.where(c <= r, u, NEG)
            _softmax_step(u, v, m_scr, l_scr, acc_scr, a)

    def body(j, carry):
        step(j, False)
        return carry

    n_full = (i * tq) // tk
    lax.fori_loop(0, n_full, body, 0)
    step(n_full, True)
    o_ref[...] = jnp.where(lo, acc_scr[0] / l_scr[0], acc_scr[1] / l_scr[1])


def _fox_prompt(fq, fkb, fvb, cum_t, *, tq, tk):
    s = fq.shape[0]
    n_pairs = N_FOX // 2
    return pl.pallas_call(
        functools.partial(_fox_prompt_kernel, tq=tq, tk=tk),
        grid=(n_pairs, s // tq),
        in_specs=[pl.BlockSpec((tq, LANES), lambda h, i: (i, h)),
                  pl.BlockSpec((s, LANES), lambda h, i: (0, h)),
                  pl.BlockSpec((s, LANES), lambda h, i: (0, h)),
                  pl.BlockSpec((None, 2, s), lambda h, i: (h, 0, 0))],
        out_specs=pl.BlockSpec((tq, LANES), lambda h, i: (i, h)),
        out_shape=jax.ShapeDtypeStruct((s, N_FOX * HEAD_DIM), F32),
        scratch_shapes=[pltpu.VMEM((2, tq, 1), F32), pltpu.VMEM((2, tq, 1), F32), pltpu.VMEM((2, tq, LANES), F32)],
        compiler_params=_cparams(2),
        name="fox_prompt",
    )(fq, fkb, fvb, cum_t)


def _select_topk(keys_scr, n_chunks, chunk, rows, topk, nbits):
    tiles = chunk // LANES
    lane = lax.broadcasted_iota(I32, (rows, LANES), 1)

    def count(pred):
        def body(c, acc):
            off = pl.multiple_of(c * chunk, chunk)
            for t in range(tiles):
                acc = acc + jnp.where(pred(keys_scr[:, pl.ds(off + t * LANES, LANES)]), 1, 0)
            return acc
        acc = lax.fori_loop(0, n_chunks, body, jnp.zeros((rows, LANES), I32))
        return jnp.sum(acc.astype(F32), axis=1, keepdims=True)

    kf = float(topk)
    t0 = jnp.where(count(lambda b: b >= 0) >= kf, 0, INT_MIN).astype(I32)

    def bit_body(s, t):
        cand = t | lax.shift_left(jnp.int32(1), 30 - s)
        cand_b = jnp.broadcast_to(cand, (rows, LANES))
        return jnp.where(count(lambda b: b >= cand_b) >= kf, cand, t)

    thr = lax.fori_loop(0, 31, bit_body, t0)
    thr_b = jnp.broadcast_to(jnp.maximum(thr, INT_MIN + 1), (rows, LANES))

    def rank_body(c, carry):
        off = pl.multiple_of(c * chunk, chunk)
        for t in range(tiles):
            sl = pl.ds(off + t * LANES, LANES)
            blk = keys_scr[:, sl]
            pos = off + t * LANES + lane
            keys_scr[:, sl] = jnp.where(blk > thr_b, -1, jnp.where(blk == thr_b, pos, IDX_BIG))
        return carry

    lax.fori_loop(0, n_chunks, rank_body, 0)

    def idx_body(s, m):
        cand = m | lax.shift_left(jnp.int32(1), nbits - 1 - s)
        cand_b = jnp.broadcast_to(cand, (rows, LANES))
        return jnp.where(count(lambda b: b < cand_b) < kf, cand, m)

    return lax.fori_loop(0, nbits, idx_body, jnp.zeros((rows, 1), I32))


def _dsa_prompt_kernel(iq_ref, iw_ref, q_ref, ik_ref, k_ref, v_ref, o_ref,
                       keys_scr, qst_scr, m_scr, l_scr, acc_scr, *, tq, kc, topk, nbits):
    i = pl.program_id(0)
    n_chunks = (i * tq) // kc + 1
    tiles = kc // LANES
    lane = lax.broadcasted_iota(I32, (tq, LANES), 1)
    lo = lane < HEAD_DIM
    qpos = i * tq + lax.broadcasted_iota(I32, (tq, LANES), 0)

    iq = iq_ref[...]
    zero = jnp.zeros((tq, LANES), BF16)
    iqm = []
    for h in range(N_IDX):
        t = iq[:, (h // 2) * LANES:(h // 2 + 1) * LANES]
        iqm.append(jnp.where(lo, t, zero) if h % 2 == 0 else jnp.where(lo, zero, t))

    def score_body(c, carry):
        off = pl.multiple_of(c * kc, kc)
        ik = ik_ref[pl.ds(off, kc), :]
        xs = [lax.dot_general(iqm[h], ik, NT, preferred_element_type=F32) for h in range(N_IDX)]
        for t in range(tiles):
            sc = jnp.zeros((tq, LANES), F32)
            for h in range(N_IDX):
                sc = sc + jnp.maximum(xs[h][:, t * LANES:(t + 1) * LANES], 0.0) * iw_ref[:, h * LANES:(h + 1) * LANES]
            kpos = off + t * LANES + lane
            keys_scr[:, pl.ds(off + t * LANES, LANES)] = jnp.where(kpos <= qpos, _sortable(sc), INT_MIN)
        return carry

    lax.fori_loop(0, n_chunks, score_body, 0)
    m_sel = _select_topk(keys_scr, n_chunks, kc, tq, topk, nbits)
    m_b = jnp.broadcast_to(m_sel, (tq, LANES))

    q = q_ref[...]
    for n in range(N_DSA_KV):
        for g in range(DSA_GROUP):
            t = q[:, g * LANES:(g + 1) * LANES]
            qst_scr[n, g * tq:(g + 1) * tq, :] = jnp.where(lo, t, zero) if n == 0 else jnp.where(lo, zero, t)
    m_scr[...] = jnp.full(m_scr.shape, NEG, F32)
    l_scr[...] = jnp.zeros(l_scr.shape, F32)
    acc_scr[...] = jnp.zeros(acc_scr.shape, F32)

    def att_body(c, carry):
        off = pl.multiple_of(c * kc, kc)
        k = k_ref[pl.ds(off, kc), :]
        v = v_ref[pl.ds(off, kc), :]
        bias = jnp.concatenate(
            [jnp.where(keys_scr[:, pl.ds(off + t * LANES, LANES)] <= m_b, 0.0, NEG) for t in range(tiles)], axis=1)
        bias = jnp.concatenate([bias] * DSA_GROUP, axis=0)
        for n in range(N_DSA_KV):
            u = lax.dot_general(qst_scr[n], k, NT, preferred_element_type=F32) + bias
            _softmax_step(u, v, m_scr, l_scr, acc_scr, n)
        return carry

    lax.fori_loop(0, n_chunks, att_body, 0)
    for g in range(DSA_GROUP):
        rows = slice(g * tq, (g + 1) * tq)
        o_ref[:, g * LANES:(g + 1) * LANES] = jnp.where(
            lo, acc_scr[0, rows, :] / l_scr[0, rows, :], acc_scr[1, rows, :] / l_scr[1, rows, :])


def _dsa_prompt(iq, iw, q, ik2, kb, vb, *, tq, kc, topk):
    s = q.shape[0]
    row = lambda w: pl.BlockSpec((tq, w), lambda i: (i, 0))
    full = lambda: pl.BlockSpec((s, LANES), lambda i: (0, 0))
    return pl.pallas_call(
        functools.partial(_dsa_prompt_kernel, tq=tq, kc=kc, topk=topk, nbits=max(1, (s - 1).bit_length())),
        grid=(s // tq,),
        in_specs=[row(N_IDX * IDX_DIM), row(N_IDX * LANES), row(N_DSA_HEADS * HEAD_DIM), full(), full(), full()],
        out_specs=row(N_DSA_HEADS * HEAD_DIM),
        out_shape=jax.ShapeDtypeStruct((s, N_DSA_HEADS * HEAD_DIM), F32),
        scratch_shapes=[pltpu.VMEM((tq, s), I32),
                        pltpu.VMEM((N_DSA_KV, DSA_GROUP * tq, LANES), BF16),
                        pltpu.VMEM((N_DSA_KV, DSA_GROUP * tq, 1), F32),
                        pltpu.VMEM((N_DSA_KV, DSA_GROUP * tq, 1), F32),
                        pltpu.VMEM((N_DSA_KV, DSA_GROUP * tq, LANES), F32)],
        compiler_params=_cparams(1),
        name="dsa_prompt",
    )(iq, iw, q, ik2, kb, vb)


def _mlp_kernel(x_ref, dsa_ref, fox_ref, wod_ref, wof_ref, g_ref, wu_ref, wd_ref, gf_ref, o_ref, *, final):
    h = (x_ref[...]
         + jnp.dot(dsa_ref[...].astype(BF16), wod_ref[...], preferred_element_type=F32)
         + jnp.dot(fox_ref[...].astype(BF16), wof_ref[...], preferred_element_type=F32))
    ms = jnp.mean(h * h, axis=-1, keepdims=True)
    hn = (h * lax.rsqrt(ms + EPS) * g_ref[...]).astype(BF16)
    a = jnp.maximum(jnp.dot(hn, wu_ref[...], preferred_element_type=F32), 0.0)
    out = h + jnp.dot((a * a).astype(BF16), wd_ref[...], preferred_element_type=F32)
    if final:
        ms = jnp.mean(out * out, axis=-1, keepdims=True)
        out = out * lax.rsqrt(ms + EPS) * gf_ref[...]
    o_ref[...] = out


def _mlp(x2d, dsa, fox, wod, wof, gain, wu, wd, gain_f, *, final, tm):
    n, d = x2d.shape
    row = lambda w: pl.BlockSpec((tm, w), lambda i: (i, 0))
    const = lambda a: pl.BlockSpec(a.shape, lambda i: (0, 0), pipeline_mode=pl.Buffered(1))
    return pl.pallas_call(
        functools.partial(_mlp_kernel, final=final),
        grid=(n // tm,),
        in_specs=[row(d), row(dsa.shape[1]), row(fox.shape[1]), const(wod), const(wof), const(gain),
                  const(wu), const(wd), const(gain_f)],
        out_specs=row(d),
        out_shape=jax.ShapeDtypeStruct((n, d), F32),
        compiler_params=_cparams(1),
        name="mlp",
    )(x2d, dsa, fox, wod, wof, gain, wu, wd, gain_f)


def _sample_fox_kernel(pt_ref, qf_ref, fkn_ref, fvn_ref, lnt_ref, iqs_ref, iws_ref, *rest, pages, n_new):
    ik_refs = rest[0:pages]
    fk_refs = rest[pages:2 * pages]
    fv_refs = rest[2 * pages:3 * pages]
    lf_refs = rest[3 * pages:4 * pages]
    o_ref, sc_ref, qbd_scr, m_scr, l_scr, acc_scr, carry_scr = rest[4 * pages:]
    j = pl.program_id(1)
    page = ik_refs[0].shape[1]

    @pl.when(j == 0)
    def _():
        m_scr[...] = jnp.full(m_scr.shape, NEG, F32)
        l_scr[...] = jnp.zeros(l_scr.shape, F32)
        acc_scr[...] = jnp.zeros(acc_scr.shape, F32)
        carry_scr[...] = jnp.zeros(carry_scr.shape, F32)
        q = qf_ref[...]
        head = lax.shift_right_logical(lax.broadcasted_iota(I32, q.shape, 1), HEAD_DIM.bit_length() - 1)
        qbd_scr[...] = jnp.concatenate([jnp.where(head == h, q, 0.0) for h in range(N_FOX)], axis=0).astype(BF16)

    def by_head_rows(x):
        return jnp.concatenate([jnp.broadcast_to(x[h:h + 1, :], (SUBLANES, page)) for h in range(N_FOX)], axis=0)

    r_i = lax.broadcasted_iota(I32, (page, 2 * page), 0)
    c_i = lax.broadcasted_iota(I32, (page, 2 * page), 1)
    scan_w = jnp.where(r_i <= c_i, 1.0, 0.0).astype(BF16)

    def lane_cumsum(x):
        out = None
        for p in _split3(x):
            t = jnp.dot(p, scan_w, preferred_element_type=F32)
            out = t if out is None else out + t
        return out

    for r in range(pages):
        rel = jnp.dot(iqs_ref[...], ik_refs[r][...].astype(BF16), preferred_element_type=F32)
        rel = jnp.maximum(rel, 0.0) * iws_ref[...]
        sc = rel[0:SUBLANES]
        for h in range(1, N_IDX):
            sc = sc + rel[h * SUBLANES:(h + 1) * SUBLANES]
        sc_ref[:, r * page:(r + 1) * page] = sc

        cs = lane_cumsum(lf_refs[r][...])
        cum = (carry_scr[...] + cs[:, :page]) * LOG2E
        carry_scr[...] = carry_scr[...] + cs[:, page:]

        u = jnp.dot(qbd_scr[...], fk_refs[r][...].astype(BF16), preferred_element_type=F32) - by_head_rows(cum)
        _softmax_step(u, fv_refs[r][...].astype(BF16), m_scr, l_scr, acc_scr, 0, v_is_transposed=True)

    @pl.when(j == pl.num_programs(1) - 1)
    def _():
        cum = (carry_scr[...] + lane_cumsum(lnt_ref[...])[:, :page]) * LOG2E
        rows = N_FOX * SUBLANES
        tok = lax.broadcasted_iota(I32, (rows, page), 0) & (SUBLANES - 1)
        col = lax.broadcasted_iota(I32, (rows, page), 1)
        valid = (col < n_new) & (col <= tok)
        u = jnp.dot(qbd_scr[...], fkn_ref[...], preferred_element_type=F32) - by_head_rows(cum)
        _softmax_step(jnp.where(valid, u, NEG), fvn_ref[...], m_scr, l_scr, acc_scr, 0, v_is_transposed=True)
        for h in range(N_FOX):
            rs = slice(h * SUBLANES, (h + 1) * SUBLANES)
            o_ref[h] = acc_scr[0, rs, h * HEAD_DIM:(h + 1) * HEAD_DIM] / l_scr[0, rs, :]


def _sample_fox(page_table, qf, fkn, fvn, lnt, iqs, iws, c_ik, c_fk, c_fv, c_lf, layer, *, pages, n_new):
    nb, n_pages = page_table.shape
    page = c_ik.shape[3]
    per_b = lambda shape: pl.BlockSpec((None,) + shape, lambda b, j, pt: (b,) + (0,) * len(shape))

    def paged(arr, r):
        blk = arr.shape[2:]
        return pl.BlockSpec((None, None) + blk,
                            lambda b, j, pt: (layer, pt[b, j * pages + r]) + (0,) * len(blk))

    in_specs = [per_b(qf.shape[1:]), per_b(fkn.shape[1:]), per_b(fvn.shape[1:]), per_b(lnt.shape[1:]),
                per_b(iqs.shape[1:]), per_b(iws.shape[1:])]
    operands = [qf, fkn, fvn, lnt, iqs, iws]
    for arr in (c_ik, c_fk, c_fv, c_lf):
        for r in range(pages):
            in_specs.append(paged(arr, r))
            operands.append(arr)
    grid_spec = pltpu.PrefetchScalarGridSpec(
        num_scalar_prefetch=1,
        grid=(nb, n_pages // pages),
        in_specs=in_specs,
        out_specs=[pl.BlockSpec((None, N_FOX, SUBLANES, HEAD_DIM), lambda b, j, pt: (b, 0, 0, 0)),
                   pl.BlockSpec((None, SUBLANES, pages * page), lambda b, j, pt: (b, 0, j))],
        scratch_shapes=[pltpu.VMEM((N_FOX * SUBLANES, N_FOX * HEAD_DIM), BF16),
                        pltpu.VMEM((1, N_FOX * SUBLANES, 1), F32), pltpu.VMEM((1, N_FOX * SUBLANES, 1), F32),
                        pltpu.VMEM((1, N_FOX * SUBLANES, N_FOX * HEAD_DIM), F32), pltpu.VMEM((N_FOX, page), F32)],
    )
    return pl.pallas_call(
        functools.partial(_sample_fox_kernel, pages=pages, n_new=n_new),
        grid_spec=grid_spec,
        out_shape=[jax.ShapeDtypeStruct((nb, N_FOX, SUBLANES, HEAD_DIM), F32),
                   jax.ShapeDtypeStruct((nb, SUBLANES, n_pages * page), F32)],
        compiler_params=_cparams(2),
        name="sample_fox",
    )(page_table, *operands)


def _sample_dsa_kernel(pt_ref, sc_ref, iqs_ref, iws_ref, ikn_ref, qd_ref, kdn_ref, vdn_ref, *rest,
                       pages, n_new, topk, chunk, nbits):
    k_refs = rest[0:pages]
    v_refs = rest[pages:2 * pages]
    o_ref, keys_scr, bias_scr, qbd_scr, m_scr, l_scr, acc_scr = rest[2 * pages:]
    j = pl.program_id(1)
    page = ikn_ref.shape[1]
    n_past = sc_ref.shape[1]
    n_rep = N_DSA_KV * DSA_GROUP

    @pl.when(j == 0)
    def _():
        m_scr[...] = jnp.full(m_scr.shape, NEG, F32)
        l_scr[...] = jnp.zeros(l_scr.shape, F32)
        acc_scr[...] = jnp.zeros(acc_scr.shape, F32)
        q = qd_ref[...]
        half = lax.shift_right_logical(lax.broadcasted_iota(I32, (SUBLANES, LANES), 1), HEAD_DIM.bit_length() - 1)
        qbd_scr[...] = jnp.concatenate(
            [jnp.where(half == n, q[:, g * LANES:(g + 1) * LANES], 0.0)
             for n in range(N_DSA_KV) for g in range(DSA_GROUP)], axis=0).astype(BF16)
        for c in range(n_past // LANES):
            keys_scr[:, c * LANES:(c + 1) * LANES] = _sortable(sc_ref[:, c * LANES:(c + 1) * LANES])
        rel = jnp.maximum(jnp.dot(iqs_ref[...], ikn_ref[...], preferred_element_type=F32), 0.0) * iws_ref[...]
        sc = rel[0:SUBLANES]
        for h in range(1, N_IDX):
            sc = sc + rel[h * SUBLANES:(h + 1) * SUBLANES]
        row = lax.broadcasted_iota(I32, (SUBLANES, page), 0)
        col = lax.broadcasted_iota(I32, (SUBLANES, page), 1)
        keys_scr[:, n_past:n_past + page] = jnp.where((col < n_new) & (col <= row), _sortable(sc), INT_MIN)
        n_total = n_past + page
        n_pad = keys_scr.shape[1]
        for c in range(n_total // LANES, n_pad // LANES):
            keys_scr[:, c * LANES:(c + 1) * LANES] = jnp.full((SUBLANES, LANES), INT_MIN, I32)
        m_sel = _select_topk(keys_scr, n_pad // chunk, chunk, SUBLANES, topk, nbits)
        m_b = jnp.broadcast_to(m_sel, (SUBLANES, LANES))
        for c in range(n_total // LANES):
            sl = slice(c * LANES, (c + 1) * LANES)
            bias_scr[:, sl] = jnp.where(keys_scr[:, sl] <= m_b, 0.0, NEG)

    def attend(kp, vp, bias):
        u = jnp.dot(qbd_scr[...], kp, preferred_element_type=F32) + jnp.concatenate([bias] * n_rep, axis=0)
        _softmax_step(u, vp, m_scr, l_scr, acc_scr, 0, v_is_transposed=True)

    for r in range(pages):
        off = pl.multiple_of((j * pages + r) * page, page)
        attend(k_refs[r][...].astype(BF16), v_refs[r][...].astype(BF16), bias_scr[:, pl.ds(off, page)])

    @pl.when(j == pl.num_programs(1) - 1)
    def _():
        attend(kdn_ref[...], vdn_ref[...], bias_scr[:, n_past:n_past + page])
        rows = DSA_GROUP * SUBLANES
        for n in range(N_DSA_KV):
            rs = slice(n * rows, (n + 1) * rows)
            o_ref[n] = acc_scr[0, rs, n * HEAD_DIM:(n + 1) * HEAD_DIM] / l_scr[0, rs, :]


def _sample_dsa(page_table, scores, iqs, iws, ikn, qd, kdn, vdn, c_k, c_v, layer, *, pages, n_new, topk):
    nb, n_pages = page_table.shape
    page = ikn.shape[2]
    n_past = scores.shape[2]
    n_total = n_past + page
    chunk = 16 * LANES
    n_pad = -(-n_total // chunk) * chunk
    rows_q = DSA_GROUP * SUBLANES
    per_b = lambda shape: pl.BlockSpec((None,) + shape, lambda b, j, pt: (b,) + (0,) * len(shape))

    def paged(arr, r):
        blk = arr.shape[2:]
        return pl.BlockSpec((None, None) + blk,
                            lambda b, j, pt: (layer, pt[b, j * pages + r]) + (0,) * len(blk))

    in_specs = [per_b(a.shape[1:]) for a in (scores, iqs, iws, ikn, qd, kdn, vdn)]
    operands = [scores, iqs, iws, ikn, qd, kdn, vdn]
    for arr in (c_k, c_v):
        for r in range(pages):
            in_specs.append(paged(arr, r))
            operands.append(arr)
    grid_spec = pltpu.PrefetchScalarGridSpec(
        num_scalar_prefetch=1,
        grid=(nb, n_pages // pages),
        in_specs=in_specs,
        out_specs=pl.BlockSpec((None, N_DSA_KV, rows_q, HEAD_DIM), lambda b, j, pt: (b, 0, 0, 0)),
        scratch_shapes=[pltpu.VMEM((SUBLANES, n_pad), I32), pltpu.VMEM((SUBLANES, n_total), F32),
                        pltpu.VMEM((N_DSA_KV * rows_q, N_DSA_KV * HEAD_DIM), BF16),
                        pltpu.VMEM((1, N_DSA_KV * rows_q, 1), F32), pltpu.VMEM((1, N_DSA_KV * rows_q, 1), F32),
                        pltpu.VMEM((1, N_DSA_KV * rows_q, N_DSA_KV * HEAD_DIM), F32)],
    )
    return pl.pallas_call(
        functools.partial(_sample_dsa_kernel, pages=pages, n_new=n_new, topk=topk, chunk=chunk,
                          nbits=max(1, (n_total - 1).bit_length())),
        grid_spec=grid_spec,
        out_shape=jax.ShapeDtypeStruct((nb, N_DSA_KV, rows_q, HEAD_DIM), F32),
        compiler_params=_cparams(2),
        name="sample_dsa",
    )(page_table, *operands)


def _rope_tables(pos):
    half = ROPE_DIMS // 2
    inv_freq = jnp.power(ROPE_THETA, -jnp.arange(half, dtype=F32) * 2.0 / ROPE_DIMS)
    ang = pos.astype(F32)[:, None] * inv_freq[None, :]
    cos, sin = jnp.cos(ang), jnp.sin(ang)
    n = pos.shape[0]
    ones = jnp.ones((n, HEAD_DIM - ROPE_DIMS), F32)
    zeros = jnp.zeros((n, HEAD_DIM - ROPE_DIMS), F32)
    zh = jnp.zeros((n, half), F32)
    rep = LANES // HEAD_DIM
    cos_t = jnp.tile(jnp.concatenate([cos, cos, ones], axis=1), (1, rep))
    sa_t = jnp.tile(jnp.concatenate([-sin, zh, zeros], axis=1), (1, rep))
    sb_t = jnp.tile(jnp.concatenate([zh, sin, zeros], axis=1), (1, rep))
    return cos_t, sa_t, sb_t


def _layer_weights(w_in_l, b_f_l, w_o_l):
    d = w_in_l.shape[0]
    dq, dkv, iq, fw = N_DSA_HEADS * HEAD_DIM, N_DSA_KV * HEAD_DIM, N_IDX * IDX_DIM, N_FOX * HEAD_DIM
    cuts = np.cumsum([dq, dkv, dkv, iq, IDX_DIM, N_IDX, fw, fw, fw, N_FOX])[:-1].tolist()
    wq, wk, wv, wiq, wik, wiw, wfq, wfk, wfv, wfg = jnp.split(w_in_l, cuts, axis=1)
    wq = wq.reshape(d, N_DSA_KV, DSA_GROUP, HEAD_DIM).transpose(0, 2, 1, 3).reshape(d, dq)
    small = jnp.concatenate([wiw, wfg, jnp.zeros((d, LANES - N_IDX - N_FOX), w_in_l.dtype)], axis=1)
    iw_rep = jnp.repeat(wiw, LANES, axis=1)
    w_all = jnp.concatenate([wq, wk, wiq, wik, wik, wv, wfq, wfk, wfv, small, iw_rep], axis=1).astype(BF16)
    assert w_all.shape[1] == N_TILES * LANES
    bias_row = jnp.zeros((1, LANES), F32).at[0, SM_FG0:SM_FG0 + N_FOX].set(b_f_l.astype(F32))
    wod = w_o_l[:dq].reshape(N_DSA_KV, DSA_GROUP, HEAD_DIM, -1).transpose(1, 0, 2, 3).reshape(dq, -1).astype(BF16)
    wof = w_o_l[dq:].astype(BF16)
    return w_all, bias_row, wod, wof


def _pad_axis(x, axis, size):
    pad = [(0, 0)] * x.ndim
    pad[axis] = (0, size - x.shape[axis])
    return jnp.pad(x, pad)


def kernel(x_prompt, x_sample, cache_dsa_k, cache_dsa_v, cache_idx_k, cache_fox_k, cache_fox_v, cache_fox_logf,
           page_table, attn_norm, w_in, b_f, w_o, mlp_norm, w_up, w_down, final_norm):
    B, S, D = x_prompt.shape
    NB, T, _ = x_sample.shape
    depth, n_pool, page = cache_dsa_k.shape[0], cache_dsa_k.shape[1], cache_dsa_k.shape[2]
    n_pages = page_table.shape[1]
    past_len = n_pages * page
    assert B == 1 and page == LANES and T <= SUBLANES
    topk_p = min(MAX_TOPK, S // 4)
    topk_s = min(MAX_TOPK, (past_len + T) // 4)

    tm_p = min(256, S)
    tq_fox = min(512, S)
    tk_fox = min(1024, S)
    tq_dsa = min(128, S)
    kc_dsa = min(1024, S)
    pages_per_step = 4 if n_pages % 4 == 0 else 1
    ns = NB * T

    tabs_p = _rope_tables(jnp.arange(S, dtype=I32))
    tabs_s = _rope_tables(past_len + jnp.tile(jnp.arange(T, dtype=I32), NB))

    as_pages = lambda c: c.transpose(0, 1, 3, 4, 2).reshape(depth, n_pool, -1, page)
    c_dk, c_dv, c_fk, c_fv = (as_pages(c) for c in (cache_dsa_k, cache_dsa_v, cache_fox_k, cache_fox_v))
    c_ik = cache_idx_k.transpose(0, 1, 3, 2)
    c_lf = cache_fox_logf.transpose(0, 1, 3, 2)

    hp = x_prompt.reshape(S, D)
    hs = x_sample.reshape(ns, D)
    gain_f = final_norm.reshape(1, D).astype(F32)
    outs_p = {k: [] for k in ("k", "v", "ik", "fk", "fv", "lf")}
    outs_s = {k: [] for k in ("k", "v", "ik", "fk", "fv", "lf")}

    for l in range(depth):
        w_all, bias_row, wod, wof = _layer_weights(w_in[l], b_f[l], w_o[l])
        g_attn = attn_norm[l].reshape(1, D).astype(F32)
        g_mlp = mlp_norm[l].reshape(1, D).astype(F32)
        wu = w_up[l].astype(BF16)
        wd = w_down[l].astype(BF16)
        final = l == depth - 1

        pr = _project(hp, g_attn, tabs_p, w_all, bias_row, with_cum=True, tm=tm_p)
        cum_t = pr["cum"][:, SM_FG0:SM_FG0 + N_FOX].T.reshape(N_FOX // 2, 2, S)
        o_fox = _fox_prompt(pr["fq"], pr["fkb"], pr["fvb"], cum_t, tq=tq_fox, tk=tk_fox)
        o_dsa = _dsa_prompt(pr["iq"], pr["iw"], pr["q"], pr["ik2"], pr["kb"], pr["vb"],
                            tq=tq_dsa, kc=kc_dsa, topk=topk_p)
        hp = _mlp(hp, o_dsa, o_fox, wod, wof, g_mlp, wu, wd, gain_f, final=final, tm=tm_p)
        for name, key in (("k", "k"), ("v", "v"), ("ik", "ik"), ("fk", "fk"), ("fv", "fv")):
            outs_p[name].append(pr[key])
        outs_p["lf"].append(pr["sm"][:, SM_FG0:SM_FG0 + N_FOX])

        sr = _project(hs, g_attn, tabs_s, w_all, bias_row, with_cum=False, tm=ns)
        by_head = lambda a, nh: a.reshape(NB, T, nh, -1).transpose(0, 2, 1, 3)
        as_page = lambda a: _pad_axis(a.reshape(NB, T, -1).transpose(0, 2, 1), 2, page)
        tok_rows = lambda a: _pad_axis(a.astype(F32).reshape(NB, T, -1), 1, SUBLANES)
        qf = tok_rows(sr["fq"])
        fkn = as_page(sr["fkb"])
        fvn = as_page(sr["fvb"])
        lf_new = sr["sm"][:, SM_FG0:SM_FG0 + N_FOX]
        lnt = _pad_axis(lf_new.reshape(NB, T, N_FOX).transpose(0, 2, 1), 2, page)
        iqs = _pad_axis(by_head(sr["iq"], N_IDX), 2, SUBLANES).reshape(NB, N_IDX * SUBLANES, IDX_DIM)
        iws = _pad_axis(by_head(sr["iw"], N_IDX), 2, SUBLANES).reshape(NB, N_IDX * SUBLANES, LANES)
        ikn = as_page(sr["ik2"][:, :IDX_DIM])
        qd = tok_rows(sr["q"])
        kdn = as_page(sr["kb"])
        vdn = as_page(sr["vb"])

        of_s, scores = _sample_fox(page_table, qf, fkn, fvn, lnt, iqs, iws, c_ik, c_fk, c_fv, c_lf, l,
                                   pages=pages_per_step, n_new=T)
        od_s = _sample_dsa(page_table, scores, iqs, iws, ikn, qd, kdn, vdn, c_dk, c_dv, l,
                           pages=pages_per_step, n_new=T, topk=topk_s)
        o_fox_s = of_s[:, :, :T].transpose(0, 2, 1, 3).reshape(ns, N_FOX * HEAD_DIM)
        o_dsa_s = od_s.reshape(NB, N_DSA_KV, DSA_GROUP, SUBLANES, HEAD_DIM)[:, :, :, :T]
        o_dsa_s = o_dsa_s.transpose(0, 3, 2, 1, 4).reshape(ns, N_DSA_HEADS * HEAD_DIM)
        hs = _mlp(hs, o_dsa_s, o_fox_s, wod, wof, g_mlp, wu, wd, gain_f, final=final, tm=ns)
        for name, key in (("k", "k"), ("v", "v"), ("ik", "ik"), ("fk", "fk"), ("fv", "fv")):
            outs_s[name].append(sr[key])
        outs_s["lf"].append(lf_new)

    def pack(outs, lead):
        st = lambda name: jnp.stack(outs[name])
        return (st("k").reshape((depth,) + lead + (N_DSA_KV, HEAD_DIM)),
                st("v").reshape((depth,) + lead + (N_DSA_KV, HEAD_DIM)),
                st("ik").reshape((depth,) + lead + (IDX_DIM,)),
                st("fk").reshape((depth,) + lead + (N_FOX, HEAD_DIM)),
                st("fv").reshape((depth,) + lead + (N_FOX, HEAD_DIM)),
                st("lf").reshape((depth,) + lead + (N_FOX,)))

    return (hp.reshape(B, S, D), hs.reshape(NB, T, D)) + pack(outs_p, (B, S)) + pack(outs_s, (NB, T))
```

```python
import functools

import jax
import jax.numpy as jnp
import numpy as np
from jax import lax
from jax.experimental import pallas as pl
from jax.experimental.pallas import tpu as pltpu

F32, BF16, I32 = jnp.float32, jnp.bfloat16, jnp.int32

HEAD_DIM = 64
N_DSA_HEADS = 8
N_DSA_KV = 2
DSA_GROUP = N_DSA_HEADS // N_DSA_KV
N_FOX = 8
N_IDX = 4
IDX_DIM = 64
ROPE_THETA = 500000.0
ROPE_DIMS = HEAD_DIM // 4
MAX_TOPK = 256
EPS = 1e-6

LANES = 128
SUBLANES = 8
VMEM_LIMIT = 56 * 1024 * 1024
NEG = -1e30
LOG2E = 1.4426950408889634
INT_MIN = -(2 ** 31)
IDX_BIG = 2 ** 30

NT = (((1,), (1,)), ((), ()))

T_Q, T_K, T_IQ, T_IK, T_V, T_FQ, T_FK, T_FV, T_SM, T_IW, N_TILES = 0, 4, 5, 7, 8, 9, 13, 17, 21, 22, 26
SM_IW0, SM_FG0 = 0, N_IDX


def _cparams(n_grid):
    return pltpu.CompilerParams(dimension_semantics=("arbitrary",) * n_grid, vmem_limit_bytes=VMEM_LIMIT)


def _split3(x):
    hi = x.astype(BF16)
    r1 = x - hi.astype(F32)
    mid = r1.astype(BF16)
    lo = (r1 - mid.astype(F32)).astype(BF16)
    return hi, mid, lo


def _sortable(x):
    x = jnp.where(x == 0.0, 0.0, x)
    b = lax.bitcast_convert_type(x, I32)
    return b ^ ((b >> 31) & 0x7FFFFFFF)


def _softmax_step(u, v, m_ref, l_ref, acc_ref, idx, v_is_transposed=False):
    vs = v if isinstance(v, (list, tuple)) else [v]
    width = u.shape[1] // len(vs)
    m_old = m_ref[idx]
    m_new = jnp.maximum(m_old, jnp.max(u, axis=1, keepdims=True))
    alpha = jnp.exp2(m_old - m_new)
    p = jnp.exp2(u - m_new)
    l_ref[idx] = alpha * l_ref[idx] + jnp.sum(p, axis=1, keepdims=True)
    pb = p.astype(BF16)
    pv = None
    for n, vn in enumerate(vs):
        pn = pb[:, n * width:(n + 1) * width]
        if v_is_transposed:
            t = lax.dot_general(pn, vn, NT, preferred_element_type=F32)
        else:
            t = jnp.dot(pn, vn, preferred_element_type=F32)
        pv = t if pv is None else pv + t
    acc_ref[idx] = alpha * acc_ref[idx] + pv
    m_ref[idx] = m_new


_PROJ_OUTS = (("q", 512, BF16), ("k", 128, F32), ("kb", 128, BF16), ("v", 128, F32), ("vb", 128, BF16),
              ("iq", 256, BF16), ("ik", 64, F32), ("ik2", 128, BF16), ("iw", 512, F32),
              ("fq", 512, BF16), ("fk", 512, F32), ("fkb", 512, BF16), ("fv", 512, F32), ("fvb", 512, BF16),
              ("sm", 128, F32), ("cum", 128, F32))


def _proj_kernel(x_ref, g_ref, cos_ref, sa_ref, sb_ref, w_ref, bias_ref, *rest, with_cum):
    n_out = len(_PROJ_OUTS) if with_cum else len(_PROJ_OUTS) - 1
    o = dict(zip([n for n, _, _ in _PROJ_OUTS], rest[:n_out]))
    carry_ref = rest[n_out]
    tm = x_ref.shape[0]

    x = x_ref[...]
    ms = jnp.mean(x * x, axis=-1, keepdims=True)
    hn = (x * lax.rsqrt(ms + EPS) * g_ref[...]).astype(BF16)
    y = jnp.dot(hn, w_ref[...], preferred_element_type=F32)

    cos, sa, sb = cos_ref[...], sa_ref[...], sb_ref[...]

    def tile(c):
        return y[:, c * LANES:(c + 1) * LANES]

    def rope(t):
        return t * cos + pltpu.roll(t, LANES - ROPE_DIMS // 2, 1) * sa + pltpu.roll(t, ROPE_DIMS // 2, 1) * sb

    scale = HEAD_DIM ** -0.5 * LOG2E
    for c in range(4):
        o["q"][:, c * LANES:(c + 1) * LANES] = (rope(tile(T_Q + c)) * scale).astype(BF16)
    kr = rope(tile(T_K))
    o["k"][...] = kr
    o["kb"][...] = kr.astype(BF16)
    for c in range(2):
        o["iq"][:, c * LANES:(c + 1) * LANES] = rope(tile(T_IQ + c)).astype(BF16)
    ikr = rope(tile(T_IK))
    o["ik"][...] = ikr[:, :IDX_DIM]
    o["ik2"][...] = ikr.astype(BF16)
    vv = tile(T_V)
    o["v"][...] = vv
    o["vb"][...] = vv.astype(BF16)
    for c in range(4):
        sl = slice(c * LANES, (c + 1) * LANES)
        o["fq"][:, sl] = (tile(T_FQ + c) * scale).astype(BF16)
        fk = tile(T_FK + c)
        o["fk"][:, sl] = fk
        o["fkb"][:, sl] = fk.astype(BF16)
        fv = tile(T_FV + c)
        o["fv"][:, sl] = fv
        o["fvb"][:, sl] = fv.astype(BF16)
        o["iw"][:, sl] = tile(T_IW + c) * (N_IDX ** -0.5)

    s = tile(T_SM)
    lane = lax.broadcasted_iota(I32, (tm, LANES), 1)
    z = s + bias_ref[...]
    logsig = jnp.minimum(z, 0.0) - jnp.log1p(jnp.exp(-jnp.abs(z)))
    sm = jnp.where(lane < SM_FG0, s * (N_IDX ** -0.5), logsig)
    o["sm"][...] = sm

    if with_cum:
        @pl.when(pl.program_id(0) == 0)
        def _():
            carry_ref[...] = jnp.zeros(carry_ref.shape, F32)

        parts = jnp.concatenate(_split3(sm), axis=1)
        r = lax.broadcasted_iota(I32, (tm, tm), 0)
        c = lax.broadcasted_iota(I32, (tm, tm), 1)
        tri = jnp.where(c <= r, 1.0, 0.0).astype(BF16)
        cs = jnp.dot(tri, parts, preferred_element_type=F32)
        cum = cs[:, :LANES] + cs[:, LANES:2 * LANES] + cs[:, 2 * LANES:] + carry_ref[0:1, :]
        o["cum"][...] = cum * LOG2E
        carry_ref[0:1, :] = cum[tm - 1:tm, :]


def _project(x2d, gain, tabs, w_all, bias_row, *, with_cum, tm):
    n, d = x2d.shape
    defs = _PROJ_OUTS if with_cum else _PROJ_OUTS[:-1]
    row = lambda w: pl.BlockSpec((tm, w), lambda i: (i, 0))
    const = lambda shape: pl.BlockSpec(shape, lambda i: (0, 0))
    outs = pl.pallas_call(
        functools.partial(_proj_kernel, with_cum=with_cum),
        grid=(n // tm,),
        in_specs=[row(d), const((1, d)), row(LANES), row(LANES), row(LANES), const(w_all.shape), const((1, LANES))],
        out_specs=[row(w) for _, w, _ in defs],
        out_shape=[jax.ShapeDtypeStruct((n, w), dt) for _, w, dt in defs],
        scratch_shapes=[pltpu.VMEM((SUBLANES, LANES), F32)],
        compiler_params=_cparams(1),
        name="proj",
    )(x2d, gain, *tabs, w_all, bias_row)
    return dict(zip([nm for nm, _, _ in defs], outs))


def _fox_prompt_kernel(q_ref, k_ref, v_ref, c_ref, o_ref, m_scr, l_scr, acc_scr, *, tq, tk):
    i = pl.program_id(1)
    lane = lax.broadcasted_iota(I32, (tq, LANES), 1)
    lo = lane < HEAD_DIM
    q = q_ref[...]
    zero = jnp.zeros_like(q)
    qs = (jnp.where(lo, q, zero), jnp.where(lo, zero, q))
    m_scr[...] = jnp.full(m_scr.shape, NEG, F32)
    l_scr[...] = jnp.zeros(l_scr.shape, F32)
    acc_scr[...] = jnp.zeros(acc_scr.shape, F32)

    def step(j, masked):
        off = pl.multiple_of(j * tk, tk)
        k = k_ref[pl.ds(off, tk), :]
        v = v_ref[pl.ds(off, tk), :]
        for a in range(2):
            s = lax.dot_general(qs[a], k, NT, preferred_element_type=F32)
            u = s - c_ref[a:a + 1, pl.ds(off, tk)]
            if masked:
                r = i * tq + lax.broadcasted_iota(I32, (tq, tk), 0)
                c = off + lax.broadcasted_iota(I32, (tq, tk), 1)
                u = jnp.where(c <= r, u, NEG)
            _softmax_step(u, v, m_scr, l_scr, acc_scr, a)

    def body(j, carry):
        step(j, False)
        return carry

    n_full = (i * tq) // tk
    lax.fori_loop(0, n_full, body, 0)
    step(n_full, True)
    o_ref[...] = jnp.where(lo, acc_scr[0] / l_scr[0], acc_scr[1] / l_scr[1])


def _fox_prompt(fq, fkb, fvb, cum_t, *, tq, tk):
    s = fq.shape[0]
    n_pairs = N_FOX // 2
    return pl.pallas_call(
        functools.partial(_fox_prompt_kernel, tq=tq, tk=tk),
        grid=(n_pairs, s // tq),
        in_specs=[pl.BlockSpec((tq, LANES), lambda h, i: (i, h)),
                  pl.BlockSpec((s, LANES), lambda h, i: (0, h)),
                  pl.BlockSpec((s, LANES), lambda h, i: (0, h)),
                  pl.BlockSpec((None, 2, s), lambda h, i: (h, 0, 0))],
        out_specs=pl.BlockSpec((tq, LANES), lambda h, i: (i, h)),
        out_shape=jax.ShapeDtypeStruct((s, N_FOX * HEAD_DIM), F32),
        scratch_shapes=[pltpu.VMEM((2, tq, 1), F32), pltpu.VMEM((2, tq, 1), F32), pltpu.VMEM((2, tq, LANES), F32)],
        compiler_params=_cparams(2),
        name="fox_prompt",
    )(fq, fkb, fvb, cum_t)


def _select_topk(keys_scr, n_chunks, chunk, rows, topk, nbits):
    tiles = chunk // LANES
    lane = lax.broadcasted_iota(I32, (rows, LANES), 1)

    def count(pred):
        def body(c, acc):
            off = pl.multiple_of(c * chunk, chunk)
            for t in range(tiles):
                acc = acc + jnp.where(pred(keys_scr[:, pl.ds(off + t * LANES, LANES)]), 1, 0)
            return acc
        acc = lax.fori_loop(0, n_chunks, body, jnp.zeros((rows, LANES), I32))
        return jnp.sum(acc.astype(F32), axis=1, keepdims=True)

    kf = float(topk)
    t0 = jnp.where(count(lambda b: b >= 0) >= kf, 0, INT_MIN).astype(I32)

    def bit_body(s, t):
        cand = t | lax.shift_left(jnp.int32(1), 30 - s)
        cand_b = jnp.broadcast_to(cand, (rows, LANES))
        return jnp.where(count(lambda b: b >= cand_b) >= kf, cand, t)

    thr = lax.fori_loop(0, 31, bit_body, t0)
    thr_b = jnp.broadcast_to(jnp.maximum(thr, INT_MIN + 1), (rows, LANES))

    def rank_body(c, carry):
        off = pl.multiple_of(c * chunk, chunk)
        for t in range(tiles):
            sl = pl.ds(off + t * LANES, LANES)
            blk = keys_scr[:, sl]
            pos = off + t * LANES + lane
            keys_scr[:, sl] = jnp.where(blk > thr_b, -1, jnp.where(blk == thr_b, pos, IDX_BIG))
        return carry

    lax.fori_loop(0, n_chunks, rank_body, 0)

    def idx_body(s, m):
        cand = m | lax.shift_left(jnp.int32(1), nbits - 1 - s)
        cand_b = jnp.broadcast_to(cand, (rows, LANES))
        return jnp.where(count(lambda b: b < cand_b) < kf, cand, m)

    return lax.fori_loop(0, nbits, idx_body, jnp.zeros((rows, 1), I32))


def _dsa_prompt_kernel(iq_ref, iw_ref, q_ref, ik_ref, k_ref, v_ref, o_ref,
                       keys_scr, qst_scr, m_scr, l_scr, acc_scr, *, tq, kc, topk, nbits):
    i = pl.program_id(0)
    n_chunks = (i * tq) // kc + 1
    tiles = kc // LANES
    lane = lax.broadcasted_iota(I32, (tq, LANES), 1)
    lo = lane < HEAD_DIM
    qpos = i * tq + lax.broadcasted_iota(I32, (tq, LANES), 0)

    iq = iq_ref[...]
    zero = jnp.zeros((tq, LANES), BF16)
    iqm = []
    for h in range(N_IDX):
        t = iq[:, (h // 2) * LANES:(h // 2 + 1) * LANES]
        iqm.append(jnp.where(lo, t, zero) if h % 2 == 0 else jnp.where(lo, zero, t))

    def score_body(c, carry):
        off = pl.multiple_of(c * kc, kc)
        ik = ik_ref[pl.ds(off, kc), :]
        xs = [lax.dot_general(iqm[h], ik, NT, preferred_element_type=F32) for h in range(N_IDX)]
        for t in range(tiles):
            sc = jnp.zeros((tq, LANES), F32)
            for h in range(N_IDX):
                sc = sc + jnp.maximum(xs[h][:, t * LANES:(t + 1) * LANES], 0.0) * iw_ref[:, h * LANES:(h + 1) * LANES]
            kpos = off + t * LANES + lane
            keys_scr[:, pl.ds(off + t * LANES, LANES)] = jnp.where(kpos <= qpos, _sortable(sc), INT_MIN)
        return carry

    lax.fori_loop(0, n_chunks, score_body, 0)
    scan = 2 * kc if keys_scr.shape[1] % (2 * kc) == 0 else kc
    if scan != kc:
        @pl.when(n_chunks % 2 == 1)
        def _():
            for t in range(tiles):
                keys_scr[:, pl.ds(pl.multiple_of(n_chunks * kc, kc) + t * LANES, LANES)] = jnp.full((tq, LANES), INT_MIN, I32)
    m_sel = _select_topk(keys_scr, (n_chunks * kc + scan - 1) // scan, scan, tq, topk, nbits)
    m_b = jnp.broadcast_to(m_sel, (tq, LANES))

    q = q_ref[...]
    for n in range(N_DSA_KV):
        for g in range(DSA_GROUP):
            t = q[:, g * LANES:(g + 1) * LANES]
            qst_scr[n, g * tq:(g + 1) * tq, :] = jnp.where(lo, t, zero) if n == 0 else jnp.where(lo, zero, t)
    m_scr[...] = jnp.full(m_scr.shape, NEG, F32)
    l_scr[...] = jnp.zeros(l_scr.shape, F32)
    acc_scr[...] = jnp.zeros(acc_scr.shape, F32)

    def att_body(c, carry):
        off = pl.multiple_of(c * kc, kc)
        k = k_ref[pl.ds(off, kc), :]
        v = v_ref[pl.ds(off, kc), :]
        bias = jnp.concatenate(
            [jnp.where(keys_scr[:, pl.ds(off + t * LANES, LANES)] <= m_b, 0.0, NEG) for t in range(tiles)], axis=1)
        bias = jnp.concatenate([bias] * DSA_GROUP, axis=0)
        for n in range(N_DSA_KV):
            u = lax.dot_general(qst_scr[n], k, NT, preferred_element_type=F32) + bias
            _softmax_step(u, v, m_scr, l_scr, acc_scr, n)
        return carry

    lax.fori_loop(0, n_chunks, att_body, 0)
    for g in range(DSA_GROUP):
        rows = slice(g * tq, (g + 1) * tq)
        o_ref[:, g * LANES:(g + 1) * LANES] = jnp.where(
            lo, acc_scr[0, rows, :] / l_scr[0, rows, :], acc_scr[1, rows, :] / l_scr[1, rows, :])


def _dsa_prompt(iq, iw, q, ik2, kb, vb, *, tq, kc, topk):
    s = q.shape[0]
    row = lambda w: pl.BlockSpec((tq, w), lambda i: (i, 0))
    full = lambda: pl.BlockSpec((s, LANES), lambda i: (0, 0))
    return pl.pallas_call(
        functools.partial(_dsa_prompt_kernel, tq=tq, kc=kc, topk=topk, nbits=max(1, (s - 1).bit_length())),
        grid=(s // tq,),
        in_specs=[row(N_IDX * IDX_DIM), row(N_IDX * LANES), row(N_DSA_HEADS * HEAD_DIM), full(), full(), full()],
        out_specs=row(N_DSA_HEADS * HEAD_DIM),
        out_shape=jax.ShapeDtypeStruct((s, N_DSA_HEADS * HEAD_DIM), F32),
        scratch_shapes=[pltpu.VMEM((tq, s), I32),
                        pltpu.VMEM((N_DSA_KV, DSA_GROUP * tq, LANES), BF16),
                        pltpu.VMEM((N_DSA_KV, DSA_GROUP * tq, 1), F32),
                        pltpu.VMEM((N_DSA_KV, DSA_GROUP * tq, 1), F32),
                        pltpu.VMEM((N_DSA_KV, DSA_GROUP * tq, LANES), F32)],
        compiler_params=_cparams(1),
        name="dsa_prompt",
    )(iq, iw, q, ik2, kb, vb)


def _mlp_kernel(x_ref, dsa_ref, fox_ref, wod_ref, wof_ref, g_ref, wu_ref, wd_ref, gf_ref, o_ref, *, final):
    h = (x_ref[...]
         + jnp.dot(dsa_ref[...].astype(BF16), wod_ref[...], preferred_element_type=F32)
         + jnp.dot(fox_ref[...].astype(BF16), wof_ref[...], preferred_element_type=F32))
    ms = jnp.mean(h * h, axis=-1, keepdims=True)
    hn = (h * lax.rsqrt(ms + EPS) * g_ref[...]).astype(BF16)
    a = jnp.maximum(jnp.dot(hn, wu_ref[...], preferred_element_type=F32), 0.0)
    out = h + jnp.dot((a * a).astype(BF16), wd_ref[...], preferred_element_type=F32)
    if final:
        ms = jnp.mean(out * out, axis=-1, keepdims=True)
        out = out * lax.rsqrt(ms + EPS) * gf_ref[...]
    o_ref[...] = out


def _mlp(x2d, dsa, fox, wod, wof, gain, wu, wd, gain_f, *, final, tm):
    n, d = x2d.shape
    row = lambda w: pl.BlockSpec((tm, w), lambda i: (i, 0))
    const = lambda a: pl.BlockSpec(a.shape, lambda i: (0, 0), pipeline_mode=pl.Buffered(1))
    return pl.pallas_call(
        functools.partial(_mlp_kernel, final=final),
        grid=(n // tm,),
        in_specs=[row(d), row(dsa.shape[1]), row(fox.shape[1]), const(wod), const(wof), const(gain),
                  const(wu), const(wd), const(gain_f)],
        out_specs=row(d),
        out_shape=jax.ShapeDtypeStruct((n, d), F32),
        compiler_params=_cparams(1),
        name="mlp",
    )(x2d, dsa, fox, wod, wof, gain, wu, wd, gain_f)


def _sample_fox_kernel(pt_ref, qf_ref, fkn_ref, fvn_ref, lnt_ref, iqs_ref, iws_ref, *rest, pages, n_new):
    ik_refs = rest[0:pages]
    fk_refs = rest[pages:2 * pages]
    fv_refs = rest[2 * pages:3 * pages]
    lf_refs = rest[3 * pages:4 * pages]
    o_ref, sc_ref, qbd_scr, m_scr, l_scr, acc_scr, carry_scr = rest[4 * pages:]
    j = pl.program_id(1)
    page = ik_refs[0].shape[1]

    @pl.when(j == 0)
    def _():
        m_scr[...] = jnp.full(m_scr.shape, NEG, F32)
        l_scr[...] = jnp.zeros(l_scr.shape, F32)
        acc_scr[...] = jnp.zeros(acc_scr.shape, F32)
        carry_scr[...] = jnp.zeros(carry_scr.shape, F32)
        q = qf_ref[...]
        head = lax.shift_right_logical(lax.broadcasted_iota(I32, q.shape, 1), HEAD_DIM.bit_length() - 1)
        qbd_scr[...] = jnp.concatenate([jnp.where(head == h, q, 0.0) for h in range(N_FOX)], axis=0).astype(BF16)

    def by_head_rows(x):
        return jnp.concatenate([jnp.broadcast_to(x[h:h + 1, :], (SUBLANES, page)) for h in range(N_FOX)], axis=0)

    r_i = lax.broadcasted_iota(I32, (page, 2 * page), 0)
    c_i = lax.broadcasted_iota(I32, (page, 2 * page), 1)
    scan_w = jnp.where(r_i <= c_i, 1.0, 0.0).astype(BF16)

    def lane_cumsum(x):
        out = None
        for p in _split3(x):
            t = jnp.dot(p, scan_w, preferred_element_type=F32)
            out = t if out is None else out + t
        return out

    carry = carry_scr[...]
    us = []
    for r in range(pages):
        rel = jnp.dot(iqs_ref[...], ik_refs[r][...].astype(BF16), preferred_element_type=F32)
        rel = jnp.maximum(rel, 0.0) * iws_ref[...]
        sc = rel[0:SUBLANES]
        for h in range(1, N_IDX):
            sc = sc + rel[h * SUBLANES:(h + 1) * SUBLANES]
        sc_ref[:, r * page:(r + 1) * page] = sc

        cs = lane_cumsum(lf_refs[r][...])
        cum = (carry + cs[:, :page]) * LOG2E
        carry = carry + cs[:, page:]

        us.append(jnp.dot(qbd_scr[...], fk_refs[r][...].astype(BF16), preferred_element_type=F32) - by_head_rows(cum))
    carry_scr[...] = carry
    _softmax_step(jnp.concatenate(us, axis=1), [fv_refs[r][...].astype(BF16) for r in range(pages)],
                  m_scr, l_scr, acc_scr, 0, v_is_transposed=True)

    @pl.when(j == pl.num_programs(1) - 1)
    def _():
        cum = (carry_scr[...] + lane_cumsum(lnt_ref[...])[:, :page]) * LOG2E
        rows = N_FOX * SUBLANES
        tok = lax.broadcasted_iota(I32, (rows, page), 0) & (SUBLANES - 1)
        col = lax.broadcasted_iota(I32, (rows, page), 1)
        valid = (col < n_new) & (col <= tok)
        u = jnp.dot(qbd_scr[...], fkn_ref[...], preferred_element_type=F32) - by_head_rows(cum)
        _softmax_step(jnp.where(valid, u, NEG), fvn_ref[...], m_scr, l_scr, acc_scr, 0, v_is_transposed=True)
        for h in range(N_FOX):
            rs = slice(h * SUBLANES, (h + 1) * SUBLANES)
            o_ref[h] = acc_scr[0, rs, h * HEAD_DIM:(h + 1) * HEAD_DIM] / l_scr[0, rs, :]


def _sample_fox(page_table, qf, fkn, fvn, lnt, iqs, iws, c_ik, c_fk, c_fv, c_lf, layer, *, pages, n_new):
    nb, n_pages = page_table.shape
    page = c_ik.shape[3]
    per_b = lambda shape: pl.BlockSpec((None,) + shape, lambda b, j, pt: (b,) + (0,) * len(shape))

    def paged(arr, r):
        blk = arr.shape[2:]
        return pl.BlockSpec((None, None) + blk,
                            lambda b, j, pt: (layer, pt[b, j * pages + r]) + (0,) * len(blk))

    in_specs = [per_b(qf.shape[1:]), per_b(fkn.shape[1:]), per_b(fvn.shape[1:]), per_b(lnt.shape[1:]),
                per_b(iqs.shape[1:]), per_b(iws.shape[1:])]
    operands = [qf, fkn, fvn, lnt, iqs, iws]
    for arr in (c_ik, c_fk, c_fv, c_lf):
        for r in range(pages):
            in_specs.append(paged(arr, r))
            operands.append(arr)
    grid_spec = pltpu.PrefetchScalarGridSpec(
        num_scalar_prefetch=1,
        grid=(nb, n_pages // pages),
        in_specs=in_specs,
        out_specs=[pl.BlockSpec((None, N_FOX, SUBLANES, HEAD_DIM), lambda b, j, pt: (b, 0, 0, 0)),
                   pl.BlockSpec((None, SUBLANES, pages * page), lambda b, j, pt: (b, 0, j))],
        scratch_shapes=[pltpu.VMEM((N_FOX * SUBLANES, N_FOX * HEAD_DIM), BF16),
                        pltpu.VMEM((1, N_FOX * SUBLANES, 1), F32), pltpu.VMEM((1, N_FOX * SUBLANES, 1), F32),
                        pltpu.VMEM((1, N_FOX * SUBLANES, N_FOX * HEAD_DIM), F32), pltpu.VMEM((N_FOX, page), F32)],
    )
    return pl.pallas_call(
        functools.partial(_sample_fox_kernel, pages=pages, n_new=n_new),
        grid_spec=grid_spec,
        out_shape=[jax.ShapeDtypeStruct((nb, N_FOX, SUBLANES, HEAD_DIM), F32),
                   jax.ShapeDtypeStruct((nb, SUBLANES, n_pages * page), F32)],
        compiler_params=_cparams(2),
        name="sample_fox",
    )(page_table, *operands)


def _sample_dsa_kernel(pt_ref, sc_ref, iqs_ref, iws_ref, ikn_ref, qd_ref, kdn_ref, vdn_ref, *rest,
                       pages, n_new, topk, chunk, nbits):
    k_refs = rest[0:pages]
    v_refs = rest[pages:2 * pages]
    o_ref, keys_scr, bias_scr, qbd_scr, m_scr, l_scr, acc_scr = rest[2 * pages:]
    j = pl.program_id(1)
    page = ikn_ref.shape[1]
    n_past = sc_ref.shape[1]
    n_rep = N_DSA_KV * DSA_GROUP

    @pl.when(j == 0)
    def _():
        m_scr[...] = jnp.full(m_scr.shape, NEG, F32)
        l_scr[...] = jnp.zeros(l_scr.shape, F32)
        acc_scr[...] = jnp.zeros(acc_scr.shape, F32)
        q = qd_ref[...]
        half = lax.shift_right_logical(lax.broadcasted_iota(I32, (SUBLANES, LANES), 1), HEAD_DIM.bit_length() - 1)
        qbd_scr[...] = jnp.concatenate(
            [jnp.where(half == n, q[:, g * LANES:(g + 1) * LANES], 0.0)
             for n in range(N_DSA_KV) for g in range(DSA_GROUP)], axis=0).astype(BF16)
        for c in range(n_past // LANES):
            keys_scr[:, c * LANES:(c + 1) * LANES] = _sortable(sc_ref[:, c * LANES:(c + 1) * LANES])
        rel = jnp.maximum(jnp.dot(iqs_ref[...], ikn_ref[...], preferred_element_type=F32), 0.0) * iws_ref[...]
        sc = rel[0:SUBLANES]
        for h in range(1, N_IDX):
            sc = sc + rel[h * SUBLANES:(h + 1) * SUBLANES]
        row = lax.broadcasted_iota(I32, (SUBLANES, page), 0)
        col = lax.broadcasted_iota(I32, (SUBLANES, page), 1)
        keys_scr[:, n_past:n_past + page] = jnp.where((col < n_new) & (col <= row), _sortable(sc), INT_MIN)
        n_total = n_past + page
        n_pad = keys_scr.shape[1]
        for c in range(n_total // LANES, n_pad // LANES):
            keys_scr[:, c * LANES:(c + 1) * LANES] = jnp.full((SUBLANES, LANES), INT_MIN, I32)
        m_sel = _select_topk(keys_scr, n_pad // chunk, chunk, SUBLANES, topk, nbits)
        m_b = jnp.broadcast_to(m_sel, (SUBLANES, LANES))
        for c in range(n_total // LANES):
            sl = slice(c * LANES, (c + 1) * LANES)
            bias_scr[:, sl] = jnp.where(keys_scr[:, sl] <= m_b, 0.0, NEG)

    def attend(kps, vps, bias):
        s = jnp.concatenate([jnp.dot(qbd_scr[...], kp, preferred_element_type=F32) for kp in kps], axis=1)
        u = s + jnp.concatenate([bias] * n_rep, axis=0)
        _softmax_step(u, vps, m_scr, l_scr, acc_scr, 0, v_is_transposed=True)

    off = pl.multiple_of(j * (pages * page), pages * page)
    attend([k_refs[r][...].astype(BF16) for r in range(pages)], [v_refs[r][...].astype(BF16) for r in range(pages)],
           bias_scr[:, pl.ds(off, pages * page)])

    @pl.when(j == pl.num_programs(1) - 1)
    def _():
        attend([kdn_ref[...]], [vdn_ref[...]], bias_scr[:, n_past:n_past + page])
        rows = DSA_GROUP * SUBLANES
        for n in range(N_DSA_KV):
            rs = slice(n * rows, (n + 1) * rows)
            o_ref[n] = acc_scr[0, rs, n * HEAD_DIM:(n + 1) * HEAD_DIM] / l_scr[0, rs, :]


def _sample_dsa(page_table, scores, iqs, iws, ikn, qd, kdn, vdn, c_k, c_v, layer, *, pages, n_new, topk):
    nb, n_pages = page_table.shape
    page = ikn.shape[2]
    n_past = scores.shape[2]
    n_total = n_past + page
    chunk = 16 * LANES
    n_pad = -(-n_total // chunk) * chunk
    rows_q = DSA_GROUP * SUBLANES
    per_b = lambda shape: pl.BlockSpec((None,) + shape, lambda b, j, pt: (b,) + (0,) * len(shape))

    def paged(arr, r):
        blk = arr.shape[2:]
        return pl.BlockSpec((None, None) + blk,
                            lambda b, j, pt: (layer, pt[b, j * pages + r]) + (0,) * len(blk))

    in_specs = [per_b(a.shape[1:]) for a in (scores, iqs, iws, ikn, qd, kdn, vdn)]
    operands = [scores, iqs, iws, ikn, qd, kdn, vdn]
    for arr in (c_k, c_v):
        for r in range(pages):
            in_specs.append(paged(arr, r))
            operands.append(arr)
    grid_spec = pltpu.PrefetchScalarGridSpec(
        num_scalar_prefetch=1,
        grid=(nb, n_pages // pages),
        in_specs=in_specs,
        out_specs=pl.BlockSpec((None, N_DSA_KV, rows_q, HEAD_DIM), lambda b, j, pt: (b, 0, 0, 0)),
        scratch_shapes=[pltpu.VMEM((SUBLANES, n_pad), I32), pltpu.VMEM((SUBLANES, n_total), F32),
                        pltpu.VMEM((N_DSA_KV * rows_q, N_DSA_KV * HEAD_DIM), BF16),
                        pltpu.VMEM((1, N_DSA_KV * rows_q, 1), F32), pltpu.VMEM((1, N_DSA_KV * rows_q, 1), F32),
                        pltpu.VMEM((1, N_DSA_KV * rows_q, N_DSA_KV * HEAD_DIM), F32)],
    )
    return pl.pallas_call(
        functools.partial(_sample_dsa_kernel, pages=pages, n_new=n_new, topk=topk, chunk=chunk,
                          nbits=max(1, (n_total - 1).bit_length())),
        grid_spec=grid_spec,
        out_shape=jax.ShapeDtypeStruct((nb, N_DSA_KV, rows_q, HEAD_DIM), F32),
        compiler_params=_cparams(2),
        name="sample_dsa",
    )(page_table, *operands)


def _rope_tables(pos):
    half = ROPE_DIMS // 2
    inv_freq = jnp.power(ROPE_THETA, -jnp.arange(half, dtype=F32) * 2.0 / ROPE_DIMS)
    ang = pos.astype(F32)[:, None] * inv_freq[None, :]
    cos, sin = jnp.cos(ang), jnp.sin(ang)
    n = pos.shape[0]
    ones = jnp.ones((n, HEAD_DIM - ROPE_DIMS), F32)
    zeros = jnp.zeros((n, HEAD_DIM - ROPE_DIMS), F32)
    zh = jnp.zeros((n, half), F32)
    rep = LANES // HEAD_DIM
    cos_t = jnp.tile(jnp.concatenate([cos, cos, ones], axis=1), (1, rep))
    sa_t = jnp.tile(jnp.concatenate([-sin, zh, zeros], axis=1), (1, rep))
    sb_t = jnp.tile(jnp.concatenate([zh, sin, zeros], axis=1), (1, rep))
    return cos_t, sa_t, sb_t


def _layer_weights(w_in_l, b_f_l, w_o_l):
    d = w_in_l.shape[0]
    dq, dkv, iq, fw = N_DSA_HEADS * HEAD_DIM, N_DSA_KV * HEAD_DIM, N_IDX * IDX_DIM, N_FOX * HEAD_DIM
    cuts = np.cumsum([dq, dkv, dkv, iq, IDX_DIM, N_IDX, fw, fw, fw, N_FOX])[:-1].tolist()
    wq, wk, wv, wiq, wik, wiw, wfq, wfk, wfv, wfg = jnp.split(w_in_l, cuts, axis=1)
    wq = wq.reshape(d, N_DSA_KV, DSA_GROUP, HEAD_DIM).transpose(0, 2, 1, 3).reshape(d, dq)
    small = jnp.concatenate([wiw, wfg, jnp.zeros((d, LANES - N_IDX - N_FOX), w_in_l.dtype)], axis=1)
    iw_rep = jnp.repeat(wiw, LANES, axis=1)
    w_all = jnp.concatenate([wq, wk, wiq, wik, wik, wv, wfq, wfk, wfv, small, iw_rep], axis=1).astype(BF16)
    assert w_all.shape[1] == N_TILES * LANES
    bias_row = jnp.zeros((1, LANES), F32).at[0, SM_FG0:SM_FG0 + N_FOX].set(b_f_l.astype(F32))
    wod = w_o_l[:dq].reshape(N_DSA_KV, DSA_GROUP, HEAD_DIM, -1).transpose(1, 0, 2, 3).reshape(dq, -1).astype(BF16)
    wof = w_o_l[dq:].astype(BF16)
    return w_all, bias_row, wod, wof


def _pad_axis(x, axis, size):
    pad = [(0, 0)] * x.ndim
    pad[axis] = (0, size - x.shape[axis])
    return jnp.pad(x, pad)


def kernel(x_prompt, x_sample, cache_dsa_k, cache_dsa_v, cache_idx_k, cache_fox_k, cache_fox_v, cache_fox_logf,
           page_table, attn_norm, w_in, b_f, w_o, mlp_norm, w_up, w_down, final_norm):
    B, S, D = x_prompt.shape
    NB, T, _ = x_sample.shape
    depth, n_pool, page = cache_dsa_k.shape[0], cache_dsa_k.shape[1], cache_dsa_k.shape[2]
    n_pages = page_table.shape[1]
    past_len = n_pages * page
    assert B == 1 and page == LANES and T <= SUBLANES
    topk_p = min(MAX_TOPK, S // 4)
    topk_s = min(MAX_TOPK, (past_len + T) // 4)

    tm_p = min(256, S)
    tq_fox = min(512, S)
    tk_fox = min(1024, S)
    tq_dsa = min(128, S)
    kc_dsa = min(1024, S)
    pages_per_step = 8 if n_pages % 8 == 0 else 1
    ns = NB * T

    tabs_p = _rope_tables(jnp.arange(S, dtype=I32))
    tabs_s = _rope_tables(past_len + jnp.tile(jnp.arange(T, dtype=I32), NB))

    as_pages = lambda c: c.transpose(0, 1, 3, 4, 2).reshape(depth, n_pool, -1, page)
    c_dk, c_dv, c_fk, c_fv = (as_pages(c) for c in (cache_dsa_k, cache_dsa_v, cache_fox_k, cache_fox_v))
    c_ik = cache_idx_k.transpose(0, 1, 3, 2)
    c_lf = cache_fox_logf.transpose(0, 1, 3, 2)

    hp = x_prompt.reshape(S, D)
    hs = x_sample.reshape(ns, D)
    gain_f = final_norm.reshape(1, D).astype(F32)
    outs_p = {k: [] for k in ("k", "v", "ik", "fk", "fv", "lf")}
    outs_s = {k: [] for k in ("k", "v", "ik", "fk", "fv", "lf")}

    for l in range(depth):
        w_all, bias_row, wod, wof = _layer_weights(w_in[l], b_f[l], w_o[l])
        g_attn = attn_norm[l].reshape(1, D).astype(F32)
        g_mlp = mlp_norm[l].reshape(1, D).astype(F32)
        wu = w_up[l].astype(BF16)
        wd = w_down[l].astype(BF16)
        final = l == depth - 1

        pr = _project(hp, g_attn, tabs_p, w_all, bias_row, with_cum=True, tm=tm_p)
        cum_t = pr["cum"][:, SM_FG0:SM_FG0 + N_FOX].T.reshape(N_FOX // 2, 2, S)
        o_fox = _fox_prompt(pr["fq"], pr["fkb"], pr["fvb"], cum_t, tq=tq_fox, tk=tk_fox)
        o_dsa = _dsa_prompt(pr["iq"], pr["iw"], pr["q"], pr["ik2"], pr["kb"], pr["vb"],
                            tq=tq_dsa, kc=kc_dsa, topk=topk_p)
        hp = _mlp(hp, o_dsa, o_fox, wod, wof, g_mlp, wu, wd, gain_f, final=final, tm=tm_p)
        for name, key in (("k", "k"), ("v", "v"), ("ik", "ik"), ("fk", "fk"), ("fv", "fv")):
            outs_p[name].append(pr[key])
        outs_p["lf"].append(pr["sm"][:, SM_FG0:SM_FG0 + N_FOX])

        sr = _project(hs, g_attn, tabs_s, w_all, bias_row, with_cum=False, tm=ns)
        by_head = lambda a, nh: a.reshape(NB, T, nh, -1).transpose(0, 2, 1, 3)
        as_page = lambda a: _pad_axis(a.reshape(NB, T, -1).transpose(0, 2, 1), 2, page)
        tok_rows = lambda a: _pad_axis(a.astype(F32).reshape(NB, T, -1), 1, SUBLANES)
        qf = tok_rows(sr["fq"])
        fkn = as_page(sr["fkb"])
        fvn = as_page(sr["fvb"])
        lf_new = sr["sm"][:, SM_FG0:SM_FG0 + N_FOX]
        lnt = _pad_axis(lf_new.reshape(NB, T, N_FOX).transpose(0, 2, 1), 2, page)
        iqs = _pad_axis(by_head(sr["iq"], N_IDX), 2, SUBLANES).reshape(NB, N_IDX * SUBLANES, IDX_DIM)
        iws = _pad_axis(by_head(sr["iw"], N_IDX), 2, SUBLANES).reshape(NB, N_IDX * SUBLANES, LANES)
        ikn = as_page(sr["ik2"][:, :IDX_DIM])
        qd = tok_rows(sr["q"])
        kdn = as_page(sr["kb"])
        vdn = as_page(sr["vb"])

        of_s, scores = _sample_fox(page_table, qf, fkn, fvn, lnt, iqs, iws, c_ik, c_fk, c_fv, c_lf, l,
                                   pages=pages_per_step, n_new=T)
        od_s = _sample_dsa(page_table, scores, iqs, iws, ikn, qd, kdn, vdn, c_dk, c_dv, l,
                           pages=pages_per_step, n_new=T, topk=topk_s)
        o_fox_s = of_s[:, :, :T].transpose(0, 2, 1, 3).reshape(ns, N_FOX * HEAD_DIM)
        o_dsa_s = od_s.reshape(NB, N_DSA_KV, DSA_GROUP, SUBLANES, HEAD_DIM)[:, :, :, :T]
        o_dsa_s = o_dsa_s.transpose(0, 3, 2, 1, 4).reshape(ns, N_DSA_HEADS * HEAD_DIM)
        hs = _mlp(hs, o_dsa_s, o_fox_s, wod, wof, g_mlp, wu, wd, gain_f, final=final, tm=ns)
        for name, key in (("k", "k"), ("v", "v"), ("ik", "ik"), ("fk", "fk"), ("fv", "fv")):
            outs_s[name].append(sr[key])
        outs_s["lf"].append(lf_new)

    def pack(outs, lead):
        st = lambda name: jnp.stack(outs[name])
        return (st("k").reshape((depth,) + lead + (N_DSA_KV, HEAD_DIM)),
                st("v").reshape((depth,) + lead + (N_DSA_KV, HEAD_DIM)),
                st("ik").reshape((depth,) + lead + (IDX_DIM,)),
                st("fk").reshape((depth,) + lead + (N_FOX, HEAD_DIM)),
                st("fv").reshape((depth,) + lead + (N_FOX, HEAD_DIM)),
                st("lf").reshape((depth,) + lead + (N_FOX,)))

    return (hp.reshape(B, S, D), hs.reshape(NB, T, D)) + pack(outs_p, (B, S)) + pack(outs_s, (NB, T))
```

```python
import functools

import jax
import jax.numpy as jnp
import numpy as np
from jax import lax
from jax.experimental import pallas as pl
from jax.experimental.pallas import tpu as pltpu

F32, BF16, I32 = jnp.float32, jnp.bfloat16, jnp.int32

HEAD_DIM = 64
N_DSA_HEADS = 8
N_DSA_KV = 2
DSA_GROUP = N_DSA_HEADS // N_DSA_KV
N_FOX = 8
N_IDX = 4
IDX_DIM = 64
ROPE_THETA = 500000.0
ROPE_DIMS = HEAD_DIM // 4
MAX_TOPK = 256
EPS = 1e-6

LANES = 128
SUBLANES = 8
VMEM_LIMIT = 56 * 1024 * 1024
NEG = -1e30
LOG2E = 1.4426950408889634
INT_MIN = -(2 ** 31)
IDX_BIG = 2 ** 30

NT = (((1,), (1,)), ((), ()))

T_Q, T_K, T_IQ, T_IK, T_V, T_FQ, T_FK, T_FV, T_SM, T_IW, N_TILES = 0, 4, 5, 7, 8, 9, 13, 17, 21, 22, 26
SM_IW0, SM_FG0 = 0, N_IDX


def _cparams(n_grid):
    return pltpu.CompilerParams(dimension_semantics=("arbitrary",) * n_grid, vmem_limit_bytes=VMEM_LIMIT)


def _split3(x):
    hi = x.astype(BF16)
    r1 = x - hi.astype(F32)
    mid = r1.astype(BF16)
    lo = (r1 - mid.astype(F32)).astype(BF16)
    return hi, mid, lo


def _sortable(x):
    x = jnp.where(x == 0.0, 0.0, x)
    b = lax.bitcast_convert_type(x, I32)
    return b ^ ((b >> 31) & 0x7FFFFFFF)


def _softmax_step(u, v, m_ref, l_ref, acc_ref, idx, v_is_transposed=False):
    vs = v if isinstance(v, (list, tuple)) else [v]
    width = u.shape[1] // len(vs)
    m_old = m_ref[idx]
    m_new = jnp.maximum(m_old, jnp.max(u, axis=1, keepdims=True))
    alpha = jnp.exp2(m_old - m_new)
    p = jnp.exp2(u - m_new)
    l_ref[idx] = alpha * l_ref[idx] + jnp.sum(p, axis=1, keepdims=True)
    pb = p.astype(BF16)
    pv = None
    for n, vn in enumerate(vs):
        pn = pb[:, n * width:(n + 1) * width]
        if v_is_transposed:
            t = lax.dot_general(pn, vn, NT, preferred_element_type=F32)
        else:
            t = jnp.dot(pn, vn, preferred_element_type=F32)
        pv = t if pv is None else pv + t
    acc_ref[idx] = alpha * acc_ref[idx] + pv
    m_ref[idx] = m_new


_PROJ_OUTS = (("q", 512, BF16), ("k", 128, F32), ("kb", 128, BF16), ("v", 128, F32), ("vb", 128, BF16),
              ("iq", 256, BF16), ("ik", 64, F32), ("ik2", 128, BF16), ("iw", 512, F32),
              ("fq", 512, BF16), ("fk", 512, F32), ("fkb", 512, BF16), ("fv", 512, F32), ("fvb", 512, BF16),
              ("sm", 128, F32), ("cum", 128, F32))


def _proj_kernel(x_ref, g_ref, cos_ref, sa_ref, sb_ref, w_ref, bias_ref, *rest, with_cum):
    n_out = len(_PROJ_OUTS) if with_cum else len(_PROJ_OUTS) - 1
    o = dict(zip([n for n, _, _ in _PROJ_OUTS], rest[:n_out]))
    carry_ref = rest[n_out]
    tm = x_ref.shape[0]

    x = x_ref[...]
    ms = jnp.mean(x * x, axis=-1, keepdims=True)
    hn = (x * lax.rsqrt(ms + EPS) * g_ref[...]).astype(BF16)
    y = jnp.dot(hn, w_ref[...], preferred_element_type=F32)

    cos, sa, sb = cos_ref[...], sa_ref[...], sb_ref[...]

    def tile(c):
        return y[:, c * LANES:(c + 1) * LANES]

    def rope(t):
        return t * cos + pltpu.roll(t, LANES - ROPE_DIMS // 2, 1) * sa + pltpu.roll(t, ROPE_DIMS // 2, 1) * sb

    scale = HEAD_DIM ** -0.5 * LOG2E
    for c in range(4):
        o["q"][:, c * LANES:(c + 1) * LANES] = (rope(tile(T_Q + c)) * scale).astype(BF16)
    kr = rope(tile(T_K))
    o["k"][...] = kr
    o["kb"][...] = kr.astype(BF16)
    for c in range(2):
        o["iq"][:, c * LANES:(c + 1) * LANES] = rope(tile(T_IQ + c)).astype(BF16)
    ikr = rope(tile(T_IK))
    o["ik"][...] = ikr[:, :IDX_DIM]
    o["ik2"][...] = ikr.astype(BF16)
    vv = tile(T_V)
    o["v"][...] = vv
    o["vb"][...] = vv.astype(BF16)
    for c in range(4):
        sl = slice(c * LANES, (c + 1) * LANES)
        o["fq"][:, sl] = (tile(T_FQ + c) * scale).astype(BF16)
        fk = tile(T_FK + c)
        o["fk"][:, sl] = fk
        o["fkb"][:, sl] = fk.astype(BF16)
        fv = tile(T_FV + c)
        o["fv"][:, sl] = fv
        o["fvb"][:, sl] = fv.astype(BF16)
        o["iw"][:, sl] = tile(T_IW + c) * (N_IDX ** -0.5)

    s = tile(T_SM)
    lane = lax.broadcasted_iota(I32, (tm, LANES), 1)
    z = s + bias_ref[...]
    logsig = jnp.minimum(z, 0.0) - jnp.log1p(jnp.exp(-jnp.abs(z)))
    sm = jnp.where(lane < SM_FG0, s * (N_IDX ** -0.5), logsig)
    o["sm"][...] = sm

    if with_cum:
        @pl.when(pl.program_id(0) == 0)
        def _():
            carry_ref[...] = jnp.zeros(carry_ref.shape, F32)

        parts = jnp.concatenate(_split3(sm), axis=1)
        r = lax.broadcasted_iota(I32, (tm, tm), 0)
        c = lax.broadcasted_iota(I32, (tm, tm), 1)
        tri = jnp.where(c <= r, 1.0, 0.0).astype(BF16)
        cs = jnp.dot(tri, parts, preferred_element_type=F32)
        cum = cs[:, :LANES] + cs[:, LANES:2 * LANES] + cs[:, 2 * LANES:] + carry_ref[0:1, :]
        o["cum"][...] = cum * LOG2E
        carry_ref[0:1, :] = cum[tm - 1:tm, :]


def _project(x2d, gain, tabs, w_all, bias_row, *, with_cum, tm):
    n, d = x2d.shape
    defs = _PROJ_OUTS if with_cum else _PROJ_OUTS[:-1]
    row = lambda w: pl.BlockSpec((tm, w), lambda i: (i, 0))
    const = lambda shape: pl.BlockSpec(shape, lambda i: (0, 0))
    outs = pl.pallas_call(
        functools.partial(_proj_kernel, with_cum=with_cum),
        grid=(n // tm,),
        in_specs=[row(d), const((1, d)), row(LANES), row(LANES), row(LANES), const(w_all.shape), const((1, LANES))],
        out_specs=[row(w) for _, w, _ in defs],
        out_shape=[jax.ShapeDtypeStruct((n, w), dt) for _, w, dt in defs],
        scratch_shapes=[pltpu.VMEM((SUBLANES, LANES), F32)],
        compiler_params=_cparams(1),
        name="proj",
    )(x2d, gain, *tabs, w_all, bias_row)
    return dict(zip([nm for nm, _, _ in defs], outs))


def _fox_prompt_kernel(q_ref, k_ref, v_ref, c_ref, o_ref, m_scr, l_scr, acc_scr, *, tq, tk):
    i = pl.program_id(1)
    lane = lax.broadcasted_iota(I32, (tq, LANES), 1)
    lo = lane < HEAD_DIM
    q = q_ref[...]
    zero = jnp.zeros_like(q)
    qs = (jnp.where(lo, q, zero), jnp.where(lo, zero, q))
    m_scr[...] = jnp.full(m_scr.shape, NEG, F32)
    l_scr[...] = jnp.zeros(l_scr.shape, F32)
    acc_scr[...] = jnp.zeros(acc_scr.shape, F32)

    def step(j, masked):
        off = pl.multiple_of(j * tk, tk)
        k = k_ref[pl.ds(off, tk), :]
        v = v_ref[pl.ds(off, tk), :]
        for a in range(2):
            s = lax.dot_general(qs[a], k, NT, preferred_element_type=F32)
            u = s - c_ref[a:a + 1, pl.ds(off, tk)]
            if masked:
                r = i * tq + lax.broadcasted_iota(I32, (tq, tk), 0)
                c = off + lax.broadcasted_iota(I32, (tq, tk), 1)
                u = jnp.where(c <= r, u, NEG)
            _softmax_step(u, v, m_scr, l_scr, acc_scr, a)

    def body(j, carry):
        step(j, False)
        return carry

    n_full = (i * tq) // tk
    lax.fori_loop(0, n_full, body, 0)
    step(n_full, True)
    o_ref[...] = jnp.where(lo, acc_scr[0] / l_scr[0], acc_scr[1] / l_scr[1])


def _fox_prompt(fq, fkb, fvb, cum_t, *, tq, tk):
    s = fq.shape[0]
    n_pairs = N_FOX // 2
    return pl.pallas_call(
        functools.partial(_fox_prompt_kernel, tq=tq, tk=tk),
        grid=(n_pairs, s // tq),
        in_specs=[pl.BlockSpec((tq, LANES), lambda h, i: (i, h)),
                  pl.BlockSpec((s, LANES), lambda h, i: (0, h)),
                  pl.BlockSpec((s, LANES), lambda h, i: (0, h)),
                  pl.BlockSpec((None, 2, s), lambda h, i: (h, 0, 0))],
        out_specs=pl.BlockSpec((tq, LANES), lambda h, i: (i, h)),
        out_shape=jax.ShapeDtypeStruct((s, N_FOX * HEAD_DIM), F32),
        scratch_shapes=[pltpu.VMEM((2, tq, 1), F32), pltpu.VMEM((2, tq, 1), F32), pltpu.VMEM((2, tq, LANES), F32)],
        compiler_params=_cparams(2),
        name="fox_prompt",
    )(fq, fkb, fvb, cum_t)


def _select_topk(keys_scr, n_chunks, chunk, rows, topk, nbits, t_init=None, n_low=32):
    tiles = chunk // LANES
    lane = lax.broadcasted_iota(I32, (rows, LANES), 1)

    def count(pred):
        def body(c, acc):
            off = pl.multiple_of(c * chunk, chunk)
            for t in range(tiles):
                acc = acc + jnp.where(pred(keys_scr[:, pl.ds(off + t * LANES, LANES)]), 1, 0)
            return acc
        acc = lax.fori_loop(0, n_chunks, body, jnp.zeros((rows, LANES), I32))
        return jnp.sum(acc.astype(F32), axis=1, keepdims=True)

    kf = float(topk)
    if t_init is None:
        t_init = jnp.full((rows, 1), INT_MIN, I32)

    def bit_body(s, t):
        cand = t ^ lax.shift_left(jnp.int32(1), n_low - 1 - s)
        cand_b = jnp.broadcast_to(cand, (rows, LANES))
        return jnp.where(count(lambda b: b >= cand_b) >= kf, cand, t)

    thr = lax.fori_loop(0, n_low, bit_body, t_init)
    thr_b = jnp.broadcast_to(jnp.maximum(thr, INT_MIN + 1), (rows, LANES))

    def rank_body(c, carry):
        off = pl.multiple_of(c * chunk, chunk)
        for t in range(tiles):
            sl = pl.ds(off + t * LANES, LANES)
            blk = keys_scr[:, sl]
            pos = off + t * LANES + lane
            keys_scr[:, sl] = jnp.where(blk > thr_b, -1, jnp.where(blk == thr_b, pos, IDX_BIG))
        return carry

    lax.fori_loop(0, n_chunks, rank_body, 0)

    def idx_body(s, m):
        cand = m | lax.shift_left(jnp.int32(1), nbits - 1 - s)
        cand_b = jnp.broadcast_to(cand, (rows, LANES))
        return jnp.where(count(lambda b: b < cand_b) < kf, cand, m)

    return lax.fori_loop(0, nbits, idx_body, jnp.zeros((rows, 1), I32))


def _dsa_prompt_kernel(iq_ref, iw_ref, q_ref, ik_ref, k_ref, v_ref, o_ref,
                       keys_scr, qst_scr, m_scr, l_scr, acc_scr, *, tq, kc, topk, nbits):
    i = pl.program_id(0)
    n_chunks = (i * tq) // kc + 1
    tiles = kc // LANES
    lane = lax.broadcasted_iota(I32, (tq, LANES), 1)
    lo = lane < HEAD_DIM
    qpos = i * tq + lax.broadcasted_iota(I32, (tq, LANES), 0)

    iq = iq_ref[...]
    zero = jnp.zeros((tq, LANES), BF16)
    iqm = []
    for h in range(N_IDX):
        t = iq[:, (h // 2) * LANES:(h // 2 + 1) * LANES]
        iqm.append(jnp.where(lo, t, zero) if h % 2 == 0 else jnp.where(lo, zero, t))

    def score_body(c, carry):
        top1, top2 = carry
        off = pl.multiple_of(c * kc, kc)
        ik = ik_ref[pl.ds(off, kc), :]
        xs = [lax.dot_general(iqm[h], ik, NT, preferred_element_type=F32) for h in range(N_IDX)]
        for t in range(tiles):
            sc = jnp.zeros((tq, LANES), F32)
            for h in range(N_IDX):
                sc = sc + jnp.maximum(xs[h][:, t * LANES:(t + 1) * LANES], 0.0) * iw_ref[:, h * LANES:(h + 1) * LANES]
            causal = off + t * LANES + lane <= qpos
            keys_scr[:, pl.ds(off + t * LANES, LANES)] = jnp.where(causal, _sortable(sc), INT_MIN)
            sc = jnp.where(causal, sc, -jnp.inf)
            top2 = jnp.maximum(top2, jnp.minimum(top1, sc))
            top1 = jnp.maximum(top1, sc)
        return top1, top2

    ninf = jnp.full((tq, LANES), -jnp.inf, F32)
    top1, top2 = lax.fori_loop(0, n_chunks, score_body, (ninf, ninf))
    scan = 2 * kc if keys_scr.shape[1] % (2 * kc) == 0 else kc
    if scan != kc:
        @pl.when(n_chunks % 2 == 1)
        def _():
            for t in range(tiles):
                keys_scr[:, pl.ds(pl.multiple_of(n_chunks * kc, kc) + t * LANES, LANES)] = jnp.full((tq, LANES), INT_MIN, I32)

    assert topk <= 2 * LANES
    upper = jnp.max(top1, axis=1, keepdims=True)
    lower = jnp.min(top2, axis=1, keepdims=True)
    hi_key = _sortable(upper)
    lo_key = jnp.where(lower == -jnp.inf, INT_MIN, _sortable(lower))
    n_diff = (32 - lax.clz(lo_key ^ hi_key)).astype(F32)

    sel_rows = min(tq, LANES)
    m_parts = []
    for r0 in range(0, tq, sel_rows):
        rs = slice(r0, r0 + sel_rows)
        n_low = jnp.max(n_diff[rs]).astype(I32)
        keep = lax.shift_left(jnp.int32(-1), jnp.minimum(n_low, 31))
        t_init = jnp.where(n_low >= 32, INT_MIN, lo_key[rs] & keep)
        m_parts.append(_select_topk(keys_scr.at[rs], (n_chunks * kc + scan - 1) // scan, scan, sel_rows, topk, nbits,
                                    t_init=t_init, n_low=n_low))
    m_sel = m_parts[0] if len(m_parts) == 1 else jnp.concatenate(m_parts, axis=0)
    m_b = jnp.broadcast_to(m_sel, (tq, LANES))

    q = q_ref[...]
    for n in range(N_DSA_KV):
        for g in range(DSA_GROUP):
            t = q[:, g * LANES:(g + 1) * LANES]
            qst_scr[n, g * tq:(g + 1) * tq, :] = jnp.where(lo, t, zero) if n == 0 else jnp.where(lo, zero, t)
    m_scr[...] = jnp.full(m_scr.shape, NEG, F32)
    l_scr[...] = jnp.zeros(l_scr.shape, F32)
    acc_scr[...] = jnp.zeros(acc_scr.shape, F32)

    def att_body(c, carry):
        off = pl.multiple_of(c * kc, kc)
        k = k_ref[pl.ds(off, kc), :]
        v = v_ref[pl.ds(off, kc), :]
        bias = jnp.concatenate(
            [jnp.where(keys_scr[:, pl.ds(off + t * LANES, LANES)] <= m_b, 0.0, NEG) for t in range(tiles)], axis=1)
        bias = jnp.concatenate([bias] * DSA_GROUP, axis=0)
        for n in range(N_DSA_KV):
            u = lax.dot_general(qst_scr[n], k, NT, preferred_element_type=F32) + bias
            _softmax_step(u, v, m_scr, l_scr, acc_scr, n)
        return carry

    lax.fori_loop(0, n_chunks, att_body, 0)
    for g in range(DSA_GROUP):
        rows = slice(g * tq, (g + 1) * tq)
        o_ref[:, g * LANES:(g + 1) * LANES] = jnp.where(
            lo, acc_scr[0, rows, :] / l_scr[0, rows, :], acc_scr[1, rows, :] / l_scr[1, rows, :])


def _dsa_prompt(iq, iw, q, ik2, kb, vb, *, tq, kc, topk):
    s = q.shape[0]
    row = lambda w: pl.BlockSpec((tq, w), lambda i: (i, 0))
    full = lambda: pl.BlockSpec((s, LANES), lambda i: (0, 0))
    return pl.pallas_call(
        functools.partial(_dsa_prompt_kernel, tq=tq, kc=kc, topk=topk, nbits=max(1, (s - 1).bit_length())),
        grid=(s // tq,),
        in_specs=[row(N_IDX * IDX_DIM), row(N_IDX * LANES), row(N_DSA_HEADS * HEAD_DIM), full(), full(), full()],
        out_specs=row(N_DSA_HEADS * HEAD_DIM),
        out_shape=jax.ShapeDtypeStruct((s, N_DSA_HEADS * HEAD_DIM), F32),
        scratch_shapes=[pltpu.VMEM((tq, s), I32),
                        pltpu.VMEM((N_DSA_KV, DSA_GROUP * tq, LANES), BF16),
                        pltpu.VMEM((N_DSA_KV, DSA_GROUP * tq, 1), F32),
                        pltpu.VMEM((N_DSA_KV, DSA_GROUP * tq, 1), F32),
                        pltpu.VMEM((N_DSA_KV, DSA_GROUP * tq, LANES), F32)],
        compiler_params=_cparams(1),
        name="dsa_prompt",
    )(iq, iw, q, ik2, kb, vb)


def _mlp_kernel(x_ref, dsa_ref, fox_ref, wod_ref, wof_ref, g_ref, wu_ref, wd_ref, gf_ref, o_ref, *, final):
    h = (x_ref[...]
         + jnp.dot(dsa_ref[...].astype(BF16), wod_ref[...], preferred_element_type=F32)
         + jnp.dot(fox_ref[...].astype(BF16), wof_ref[...], preferred_element_type=F32))
    ms = jnp.mean(h * h, axis=-1, keepdims=True)
    hn = (h * lax.rsqrt(ms + EPS) * g_ref[...]).astype(BF16)
    a = jnp.maximum(jnp.dot(hn, wu_ref[...], preferred_element_type=F32), 0.0)
    out = h + jnp.dot((a * a).astype(BF16), wd_ref[...], preferred_element_type=F32)
    if final:
        ms = jnp.mean(out * out, axis=-1, keepdims=True)
        out = out * lax.rsqrt(ms + EPS) * gf_ref[...]
    o_ref[...] = out


def _mlp(x2d, dsa, fox, wod, wof, gain, wu, wd, gain_f, *, final, tm):
    n, d = x2d.shape
    row = lambda w: pl.BlockSpec((tm, w), lambda i: (i, 0))
    const = lambda a: pl.BlockSpec(a.shape, lambda i: (0, 0), pipeline_mode=pl.Buffered(1))
    return pl.pallas_call(
        functools.partial(_mlp_kernel, final=final),
        grid=(n // tm,),
        in_specs=[row(d), row(dsa.shape[1]), row(fox.shape[1]), const(wod), const(wof), const(gain),
                  const(wu), const(wd), const(gain_f)],
        out_specs=row(d),
        out_shape=jax.ShapeDtypeStruct((n, d), F32),
        compiler_params=_cparams(1),
        name="mlp",
    )(x2d, dsa, fox, wod, wof, gain, wu, wd, gain_f)


def _sample_fox_kernel(pt_ref, qf_ref, fkn_ref, fvn_ref, lnt_ref, iqs_ref, iws_ref, *rest, pages, n_new):
    ik_refs = rest[0:pages]
    fk_refs = rest[pages:2 * pages]
    fv_refs = rest[2 * pages:3 * pages]
    lf_refs = rest[3 * pages:4 * pages]
    o_ref, sc_ref, qbd_scr, m_scr, l_scr, acc_scr, carry_scr = rest[4 * pages:]
    j = pl.program_id(1)
    page = ik_refs[0].shape[1]

    @pl.when(j == 0)
    def _():
        m_scr[...] = jnp.full(m_scr.shape, NEG, F32)
        l_scr[...] = jnp.zeros(l_scr.shape, F32)
        acc_scr[...] = jnp.zeros(acc_scr.shape, F32)
        carry_scr[...] = jnp.zeros(carry_scr.shape, F32)
        q = qf_ref[...]
        head = lax.shift_right_logical(lax.broadcasted_iota(I32, q.shape, 1), HEAD_DIM.bit_length() - 1)
        qbd_scr[...] = jnp.concatenate([jnp.where(head == h, q, 0.0) for h in range(N_FOX)], axis=0).astype(BF16)

    def by_head_rows(x):
        return jnp.concatenate([jnp.broadcast_to(x[h:h + 1, :], (SUBLANES, page)) for h in range(N_FOX)], axis=0)

    r_i = lax.broadcasted_iota(I32, (page, 2 * page), 0)
    c_i = lax.broadcasted_iota(I32, (page, 2 * page), 1)
    scan_w = jnp.where(r_i <= c_i, 1.0, 0.0).astype(BF16)

    def lane_cumsum(x):
        out = None
        for p in _split3(x):
            t = jnp.dot(p, scan_w, preferred_element_type=F32)
            out = t if out is None else out + t
        return out

    carry = carry_scr[...]
    us = []
    for r in range(pages):
        rel = jnp.dot(iqs_ref[...], ik_refs[r][...].astype(BF16), preferred_element_type=F32)
        rel = jnp.maximum(rel, 0.0) * iws_ref[...]
        sc = rel[0:SUBLANES]
        for h in range(1, N_IDX):
            sc = sc + rel[h * SUBLANES:(h + 1) * SUBLANES]
        sc_ref[:, r * page:(r + 1) * page] = sc

        cs = lane_cumsum(lf_refs[r][...])
        cum = (carry + cs[:, :page]) * LOG2E
        carry = carry + cs[:, page:]

        us.append(jnp.dot(qbd_scr[...], fk_refs[r][...].astype(BF16), preferred_element_type=F32) - by_head_rows(cum))
    carry_scr[...] = carry
    _softmax_step(jnp.concatenate(us, axis=1), [fv_refs[r][...].astype(BF16) for r in range(pages)],
                  m_scr, l_scr, acc_scr, 0, v_is_transposed=True)

    @pl.when(j == pl.num_programs(1) - 1)
    def _():
        cum = (carry_scr[...] + lane_cumsum(lnt_ref[...])[:, :page]) * LOG2E
        rows = N_FOX * SUBLANES
        tok = lax.broadcasted_iota(I32, (rows, page), 0) & (SUBLANES - 1)
        col = lax.broadcasted_iota(I32, (rows, page), 1)
        valid = (col < n_new) & (col <= tok)
        u = jnp.dot(qbd_scr[...], fkn_ref[...], preferred_element_type=F32) - by_head_rows(cum)
        _softmax_step(jnp.where(valid, u, NEG), fvn_ref[...], m_scr, l_scr, acc_scr, 0, v_is_transposed=True)
        for h in range(N_FOX):
            rs = slice(h * SUBLANES, (h + 1) * SUBLANES)
            o_ref[h] = acc_scr[0, rs, h * HEAD_DIM:(h + 1) * HEAD_DIM] / l_scr[0, rs, :]


def _sample_fox(page_table, qf, fkn, fvn, lnt, iqs, iws, c_ik, c_fk, c_fv, c_lf, layer, *, pages, n_new):
    nb, n_pages = page_table.shape
    page = c_ik.shape[3]
    per_b = lambda shape: pl.BlockSpec((None,) + shape, lambda b, j, pt: (b,) + (0,) * len(shape))

    def paged(arr, r):
        blk = arr.shape[2:]
        return pl.BlockSpec((None, None) + blk,
                            lambda b, j, pt: (layer, pt[b, j * pages + r]) + (0,) * len(blk))

    in_specs = [per_b(qf.shape[1:]), per_b(fkn.shape[1:]), per_b(fvn.shape[1:]), per_b(lnt.shape[1:]),
                per_b(iqs.shape[1:]), per_b(iws.shape[1:])]
    operands = [qf, fkn, fvn, lnt, iqs, iws]
    for arr in (c_ik, c_fk, c_fv, c_lf):
        for r in range(pages):
            in_specs.append(paged(arr, r))
            operands.append(arr)
    grid_spec = pltpu.PrefetchScalarGridSpec(
        num_scalar_prefetch=1,
        grid=(nb, n_pages // pages),
        in_specs=in_specs,
        out_specs=[pl.BlockSpec((None, N_FOX, SUBLANES, HEAD_DIM), lambda b, j, pt: (b, 0, 0, 0)),
                   pl.BlockSpec((None, SUBLANES, pages * page), lambda b, j, pt: (b, 0, j))],
        scratch_shapes=[pltpu.VMEM((N_FOX * SUBLANES, N_FOX * HEAD_DIM), BF16),
                        pltpu.VMEM((1, N_FOX * SUBLANES, 1), F32), pltpu.VMEM((1, N_FOX * SUBLANES, 1), F32),
                        pltpu.VMEM((1, N_FOX * SUBLANES, N_FOX * HEAD_DIM), F32), pltpu.VMEM((N_FOX, page), F32)],
    )
    return pl.pallas_call(
        functools.partial(_sample_fox_kernel, pages=pages, n_new=n_new),
        grid_spec=grid_spec,
        out_shape=[jax.ShapeDtypeStruct((nb, N_FOX, SUBLANES, HEAD_DIM), F32),
                   jax.ShapeDtypeStruct((nb, SUBLANES, n_pages * page), F32)],
        compiler_params=_cparams(2),
        name="sample_fox",
    )(page_table, *operands)


def _sample_dsa_kernel(pt_ref, sc_ref, iqs_ref, iws_ref, ikn_ref, qd_ref, kdn_ref, vdn_ref, *rest,
                       pages, n_new, topk, chunk, nbits):
    k_refs = rest[0:pages]
    v_refs = rest[pages:2 * pages]
    o_ref, keys_scr, bias_scr, qbd_scr, m_scr, l_scr, acc_scr = rest[2 * pages:]
    j = pl.program_id(1)
    page = ikn_ref.shape[1]
    n_past = sc_ref.shape[1]
    n_rep = N_DSA_KV * DSA_GROUP

    @pl.when(j == 0)
    def _():
        m_scr[...] = jnp.full(m_scr.shape, NEG, F32)
        l_scr[...] = jnp.zeros(l_scr.shape, F32)
        acc_scr[...] = jnp.zeros(acc_scr.shape, F32)
        q = qd_ref[...]
        half = lax.shift_right_logical(lax.broadcasted_iota(I32, (SUBLANES, LANES), 1), HEAD_DIM.bit_length() - 1)
        qbd_scr[...] = jnp.concatenate(
            [jnp.where(half == n, q[:, g * LANES:(g + 1) * LANES], 0.0)
             for n in range(N_DSA_KV) for g in range(DSA_GROUP)], axis=0).astype(BF16)
        for c in range(n_past // LANES):
            keys_scr[:, c * LANES:(c + 1) * LANES] = _sortable(sc_ref[:, c * LANES:(c + 1) * LANES])
        rel = jnp.maximum(jnp.dot(iqs_ref[...], ikn_ref[...], preferred_element_type=F32), 0.0) * iws_ref[...]
        sc = rel[0:SUBLANES]
        for h in range(1, N_IDX):
            sc = sc + rel[h * SUBLANES:(h + 1) * SUBLANES]
        row = lax.broadcasted_iota(I32, (SUBLANES, page), 0)
        col = lax.broadcasted_iota(I32, (SUBLANES, page), 1)
        keys_scr[:, n_past:n_past + page] = jnp.where((col < n_new) & (col <= row), _sortable(sc), INT_MIN)
        n_total = n_past + page
        n_pad = keys_scr.shape[1]
        for c in range(n_total // LANES, n_pad // LANES):
            keys_scr[:, c * LANES:(c + 1) * LANES] = jnp.full((SUBLANES, LANES), INT_MIN, I32)
        m_sel = _select_topk(keys_scr, n_pad // chunk, chunk, SUBLANES, topk, nbits)
        m_b = jnp.broadcast_to(m_sel, (SUBLANES, LANES))
        for c in range(n_total // LANES):
            sl = slice(c * LANES, (c + 1) * LANES)
            bias_scr[:, sl] = jnp.where(keys_scr[:, sl] <= m_b, 0.0, NEG)

    def attend(kps, vps, bias):
        s = jnp.concatenate([jnp.dot(qbd_scr[...], kp, preferred_element_type=F32) for kp in kps], axis=1)
        u = s + jnp.concatenate([bias] * n_rep, axis=0)
        _softmax_step(u, vps, m_scr, l_scr, acc_scr, 0, v_is_transposed=True)

    off = pl.multiple_of(j * (pages * page), pages * page)
    attend([k_refs[r][...].astype(BF16) for r in range(pages)], [v_refs[r][...].astype(BF16) for r in range(pages)],
           bias_scr[:, pl.ds(off, pages * page)])

    @pl.when(j == pl.num_programs(1) - 1)
    def _():
        attend([kdn_ref[...]], [vdn_ref[...]], bias_scr[:, n_past:n_past + page])
        rows = DSA_GROUP * SUBLANES
        for n in range(N_DSA_KV):
            rs = slice(n * rows, (n + 1) * rows)
            o_ref[n] = acc_scr[0, rs, n * HEAD_DIM:(n + 1) * HEAD_DIM] / l_scr[0, rs, :]


def _sample_dsa(page_table, scores, iqs, iws, ikn, qd, kdn, vdn, c_k, c_v, layer, *, pages, n_new, topk):
    nb, n_pages = page_table.shape
    page = ikn.shape[2]
    n_past = scores.shape[2]
    n_total = n_past + page
    chunk = 16 * LANES
    n_pad = -(-n_total // chunk) * chunk
    rows_q = DSA_GROUP * SUBLANES
    per_b = lambda shape: pl.BlockSpec((None,) + shape, lambda b, j, pt: (b,) + (0,) * len(shape))

    def paged(arr, r):
        blk = arr.shape[2:]
        return pl.BlockSpec((None, None) + blk,
                            lambda b, j, pt: (layer, pt[b, j * pages + r]) + (0,) * len(blk))

    in_specs = [per_b(a.shape[1:]) for a in (scores, iqs, iws, ikn, qd, kdn, vdn)]
    operands = [scores, iqs, iws, ikn, qd, kdn, vdn]
    for arr in (c_k, c_v):
        for r in range(pages):
            in_specs.append(paged(arr, r))
            operands.append(arr)
    grid_spec = pltpu.PrefetchScalarGridSpec(
        num_scalar_prefetch=1,
        grid=(nb, n_pages // pages),
        in_specs=in_specs,
        out_specs=pl.BlockSpec((None, N_DSA_KV, rows_q, HEAD_DIM), lambda b, j, pt: (b, 0, 0, 0)),
        scratch_shapes=[pltpu.VMEM((SUBLANES, n_pad), I32), pltpu.VMEM((SUBLANES, n_total), F32),
                        pltpu.VMEM((N_DSA_KV * rows_q, N_DSA_KV * HEAD_DIM), BF16),
                        pltpu.VMEM((1, N_DSA_KV * rows_q, 1), F32), pltpu.VMEM((1, N_DSA_KV * rows_q, 1), F32),
                        pltpu.VMEM((1, N_DSA_KV * rows_q, N_DSA_KV * HEAD_DIM), F32)],
    )
    return pl.pallas_call(
        functools.partial(_sample_dsa_kernel, pages=pages, n_new=n_new, topk=topk, chunk=chunk,
                          nbits=max(1, (n_total - 1).bit_length())),
        grid_spec=grid_spec,
        out_shape=jax.ShapeDtypeStruct((nb, N_DSA_KV, rows_q, HEAD_DIM), F32),
        compiler_params=_cparams(2),
        name="sample_dsa",
    )(page_table, *operands)


def _rope_tables(pos):
    half = ROPE_DIMS // 2
    inv_freq = jnp.power(ROPE_THETA, -jnp.arange(half, dtype=F32) * 2.0 / ROPE_DIMS)
    ang = pos.astype(F32)[:, None] * inv_freq[None, :]
    cos, sin = jnp.cos(ang), jnp.sin(ang)
    n = pos.shape[0]
    ones = jnp.ones((n, HEAD_DIM - ROPE_DIMS), F32)
    zeros = jnp.zeros((n, HEAD_DIM - ROPE_DIMS), F32)
    zh = jnp.zeros((n, half), F32)
    rep = LANES // HEAD_DIM
    cos_t = jnp.tile(jnp.concatenate([cos, cos, ones], axis=1), (1, rep))
    sa_t = jnp.tile(jnp.concatenate([-sin, zh, zeros], axis=1), (1, rep))
    sb_t = jnp.tile(jnp.concatenate([zh, sin, zeros], axis=1), (1, rep))
    return cos_t, sa_t, sb_t


def _layer_weights(w_in_l, b_f_l, w_o_l):
    d = w_in_l.shape[0]
    dq, dkv, iq, fw = N_DSA_HEADS * HEAD_DIM, N_DSA_KV * HEAD_DIM, N_IDX * IDX_DIM, N_FOX * HEAD_DIM
    cuts = np.cumsum([dq, dkv, dkv, iq, IDX_DIM, N_IDX, fw, fw, fw, N_FOX])[:-1].tolist()
    wq, wk, wv, wiq, wik, wiw, wfq, wfk, wfv, wfg = jnp.split(w_in_l, cuts, axis=1)
    wq = wq.reshape(d, N_DSA_KV, DSA_GROUP, HEAD_DIM).transpose(0, 2, 1, 3).reshape(d, dq)
    small = jnp.concatenate([wiw, wfg, jnp.zeros((d, LANES - N_IDX - N_FOX), w_in_l.dtype)], axis=1)
    iw_rep = jnp.repeat(wiw, LANES, axis=1)
    w_all = jnp.concatenate([wq, wk, wiq, wik, wik, wv, wfq, wfk, wfv, small, iw_rep], axis=1).astype(BF16)
    assert w_all.shape[1] == N_TILES * LANES
    bias_row = jnp.zeros((1, LANES), F32).at[0, SM_FG0:SM_FG0 + N_FOX].set(b_f_l.astype(F32))
    wod = w_o_l[:dq].reshape(N_DSA_KV, DSA_GROUP, HEAD_DIM, -1).transpose(1, 0, 2, 3).reshape(dq, -1).astype(BF16)
    wof = w_o_l[dq:].astype(BF16)
    return w_all, bias_row, wod, wof


def _pad_axis(x, axis, size):
    pad = [(0, 0)] * x.ndim
    pad[axis] = (0, size - x.shape[axis])
    return jnp.pad(x, pad)


def kernel(x_prompt, x_sample, cache_dsa_k, cache_dsa_v, cache_idx_k, cache_fox_k, cache_fox_v, cache_fox_logf,
           page_table, attn_norm, w_in, b_f, w_o, mlp_norm, w_up, w_down, final_norm):
    B, S, D = x_prompt.shape
    NB, T, _ = x_sample.shape
    depth, n_pool, page = cache_dsa_k.shape[0], cache_dsa_k.shape[1], cache_dsa_k.shape[2]
    n_pages = page_table.shape[1]
    past_len = n_pages * page
    assert B == 1 and page == LANES and T <= SUBLANES
    topk_p = min(MAX_TOPK, S // 4)
    topk_s = min(MAX_TOPK, (past_len + T) // 4)

    tm_p = min(256, S)
    tq_fox = min(512, S)
    tk_fox = min(2048, S)
    tq_dsa = min(256, S)
    kc_dsa = min(1024, S)
    pages_per_step = 8 if n_pages % 8 == 0 else 1
    ns = NB * T

    tabs_p = _rope_tables(jnp.arange(S, dtype=I32))
    tabs_s = _rope_tables(past_len + jnp.tile(jnp.arange(T, dtype=I32), NB))

    as_pages = lambda c: c.transpose(0, 1, 3, 4, 2).reshape(depth, n_pool, -1, page)
    c_dk, c_dv, c_fk, c_fv = (as_pages(c) for c in (cache_dsa_k, cache_dsa_v, cache_fox_k, cache_fox_v))
    c_ik = cache_idx_k.transpose(0, 1, 3, 2)
    c_lf = cache_fox_logf.transpose(0, 1, 3, 2)

    hp = x_prompt.reshape(S, D)
    hs = x_sample.reshape(ns, D)
    gain_f = final_norm.reshape(1, D).astype(F32)
    outs_p = {k: [] for k in ("k", "v", "ik", "fk", "fv", "lf")}
    outs_s = {k: [] for k in ("k", "v", "ik", "fk", "fv", "lf")}

    for l in range(depth):
        w_all, bias_row, wod, wof = _layer_weights(w_in[l], b_f[l], w_o[l])
        g_attn = attn_norm[l].reshape(1, D).astype(F32)
        g_mlp = mlp_norm[l].reshape(1, D).astype(F32)
        wu = w_up[l].astype(BF16)
        wd = w_down[l].astype(BF16)
        final = l == depth - 1

        pr = _project(hp, g_attn, tabs_p, w_all, bias_row, with_cum=True, tm=tm_p)
        cum_t = pr["cum"][:, SM_FG0:SM_FG0 + N_FOX].T.reshape(N_FOX // 2, 2, S)
        o_fox = _fox_prompt(pr["fq"], pr["fkb"], pr["fvb"], cum_t, tq=tq_fox, tk=tk_fox)
        o_dsa = _dsa_prompt(pr["iq"], pr["iw"], pr["q"], pr["ik2"], pr["kb"], pr["vb"],
                            tq=tq_dsa, kc=kc_dsa, topk=topk_p)
        hp = _mlp(hp, o_dsa, o_fox, wod, wof, g_mlp, wu, wd, gain_f, final=final, tm=tm_p)
        for name, key in (("k", "k"), ("v", "v"), ("ik", "ik"), ("fk", "fk"), ("fv", "fv")):
            outs_p[name].append(pr[key])
        outs_p["lf"].append(pr["sm"][:, SM_FG0:SM_FG0 + N_FOX])

        sr = _project(hs, g_attn, tabs_s, w_all, bias_row, with_cum=False, tm=ns)
        by_head = lambda a, nh: a.reshape(NB, T, nh, -1).transpose(0, 2, 1, 3)
        as_page = lambda a: _pad_axis(a.reshape(NB, T, -1).transpose(0, 2, 1), 2, page)
        tok_rows = lambda a: _pad_axis(a.astype(F32).reshape(NB, T, -1), 1, SUBLANES)
        qf = tok_rows(sr["fq"])
        fkn = as_page(sr["fkb"])
        fvn = as_page(sr["fvb"])
        lf_new = sr["sm"][:, SM_FG0:SM_FG0 + N_FOX]
        lnt = _pad_axis(lf_new.reshape(NB, T, N_FOX).transpose(0, 2, 1), 2, page)
        iqs = _pad_axis(by_head(sr["iq"], N_IDX), 2, SUBLANES).reshape(NB, N_IDX * SUBLANES, IDX_DIM)
        iws = _pad_axis(by_head(sr["iw"], N_IDX), 2, SUBLANES).reshape(NB, N_IDX * SUBLANES, LANES)
        ikn = as_page(sr["ik2"][:, :IDX_DIM])
        qd = tok_rows(sr["q"])
        kdn = as_page(sr["kb"])
        vdn = as_page(sr["vb"])

        of_s, scores = _sample_fox(page_table, qf, fkn, fvn, lnt, iqs, iws, c_ik, c_fk, c_fv, c_lf, l,
                                   pages=pages_per_step, n_new=T)
        od_s = _sample_dsa(page_table, scores, iqs, iws, ikn, qd, kdn, vdn, c_dk, c_dv, l,
                           pages=pages_per_step, n_new=T, topk=topk_s)
        o_fox_s = of_s[:, :, :T].transpose(0, 2, 1, 3).reshape(ns, N_FOX * HEAD_DIM)
        o_dsa_s = od_s.reshape(NB, N_DSA_KV, DSA_GROUP, SUBLANES, HEAD_DIM)[:, :, :, :T]
        o_dsa_s = o_dsa_s.transpose(0, 3, 2, 1, 4).reshape(ns, N_DSA_HEADS * HEAD_DIM)
        hs = _mlp(hs, o_dsa_s, o_fox_s, wod, wof, g_mlp, wu, wd, gain_f, final=final, tm=ns)
        for name, key in (("k", "k"), ("v", "v"), ("ik", "ik"), ("fk", "fk"), ("fv", "fv")):
            outs_s[name].append(sr[key])
        outs_s["lf"].append(lf_new)

    def pack(outs, lead):
        st = lambda name: jnp.stack(outs[name])
        return (st("k").reshape((depth,) + lead + (N_DSA_KV, HEAD_DIM)),
                st("v").reshape((depth,) + lead + (N_DSA_KV, HEAD_DIM)),
                st("ik").reshape((depth,) + lead + (IDX_DIM,)),
                st("fk").reshape((depth,) + lead + (N_FOX, HEAD_DIM)),
                st("fv").reshape((depth,) + lead + (N_FOX, HEAD_DIM)),
                st("lf").reshape((depth,) + lead + (N_FOX,)))

    return (hp.reshape(B, S, D), hs.reshape(NB, T, D)) + pack(outs_p, (B, S)) + pack(outs_s, (NB, T))
```
